```python
import math
import jax, jax.numpy as jnp
from jax import lax
import numpy as np

D_MODEL = 1024
BATCH = 16
SEQ = 2048
DEPTH = 4

D_MIX = D_MODEL
A_HEADS = 4
A_QK_DIM = 32
A_V_DIM = 64
B_HEADS = 4
B_DIM = 64
B_CONFIGS = ((128, 1), (512, 4), (2048, 16))
B_BLOCK = 64
C_HEADS = 8
C_NOPE = 64
C_ROPE = 32
C_V = 64
C_Q_RANK = 256
C_KV_RANK = 128
ROPE_BASE = 10000.0
D_FF = 2816
CONV_W = 3
Q_BLOCK = 128
EPS = 1e-6
NEG = -1e30

A_Q_COLS = A_HEADS * 2 * A_QK_DIM
A_K_COLS = A_HEADS * 2 * A_QK_DIM
A_V_COLS = A_HEADS * A_V_DIM
B_COLS = B_HEADS * B_DIM
IN_SPLITS = (A_Q_COLS, A_K_COLS, A_V_COLS, B_COLS, B_COLS, B_COLS,
             C_Q_RANK, C_KV_RANK, C_ROPE)
N_IN = sum(IN_SPLITS)

kernel_name = "hybrid_diff_dilated_mla_encoder"


def _alibi_slopes():
    n = A_HEADS + B_HEADS
    s = (2.0 ** (-8.0 * np.arange(1, n + 1) / n)).astype(np.float32)
    return jnp.asarray(s[0::2]), jnp.asarray(s[1::2])


def _rmsnorm(x, g):
    xf = x.astype(jnp.float32)
    y = xf * lax.rsqrt(jnp.mean(xf * xf, axis=-1, keepdims=True) + EPS)
    return (y * g.astype(jnp.float32)).astype(x.dtype)


def _heads(t, n_heads):
    b, s, _ = t.shape
    return t.reshape(b, s, n_heads, -1).transpose(0, 2, 1, 3)


def _merge_heads(t):
    b, h, s, d = t.shape
    return t.transpose(0, 2, 1, 3).reshape(b, s, h * d)


def _rope(t, cos, sin):
    half = t.shape[-1] // 2
    t1 = t[..., :half].astype(jnp.float32)
    t2 = t[..., half:].astype(jnp.float32)
    return jnp.concatenate([t1 * cos - t2 * sin, t2 * cos + t1 * sin], axis=-1).astype(t.dtype)


def _sweep_queries(block_fn, qs):
    b, h, s, _ = qs[0].shape
    nq = s // Q_BLOCK
    blocks = tuple(q.reshape(b, h, nq, Q_BLOCK, q.shape[-1]).transpose(2, 0, 1, 3, 4) for q in qs)
    out = lax.map(lambda a: block_fn(a[0], *a[1]), (jnp.arange(nq), blocks))
    return out.transpose(1, 2, 0, 3, 4).reshape(b, h, s, -1)


def _diff_attention(q1, q2, k1, k2, v, lam, slopes):
    s_len = k1.shape[2]
    scale = A_QK_DIM ** -0.5
    kpos = jnp.arange(s_len)

    def block(i, qb1, qb2):
        qpos = i * Q_BLOCK + jnp.arange(Q_BLOCK)
        dist = jnp.abs(qpos[:, None] - kpos[None, :]).astype(jnp.float32)
        bias = -slopes[:, None, None] * dist
        s1 = jnp.einsum('bhqd,bhkd->bhqk', qb1, k1).astype(jnp.float32) * scale + bias
        s2 = jnp.einsum('bhqd,bhkd->bhqk', qb2, k2).astype(jnp.float32) * scale + bias
        p = jax.nn.softmax(s1, axis=-1) - lam * jax.nn.softmax(s2, axis=-1)
        return jnp.einsum('bhqk,bhkd->bhqd', p.astype(v.dtype), v)

    return _sweep_queries(block, (q1, q2))


def _dilated_branch(q, k, v, window, dil, slopes):
    b, h, s_len, d = q.shape
    half = window // (2 * dil)
    L = s_len // dil
    nb = -(-L // B_BLOCK)
    lp = nb * B_BLOCK
    scale = d ** -0.5

    def to_sub(t):
        t = t.reshape(b, h, L, dil, d).transpose(0, 1, 3, 2, 4)
        return jnp.pad(t, ((0, 0), (0, 0), (0, 0), (0, lp - L), (0, 0)))

    def band(t):
        tb = t.reshape(b, h, dil, nb, B_BLOCK, d)
        tb = jnp.pad(tb, ((0, 0), (0, 0), (0, 0), (1, 1), (0, 0), (0, 0)))
        return jnp.concatenate([tb[:, :, :, :-2], tb[:, :, :, 1:-1], tb[:, :, :, 2:]], axis=4)

    qb = to_sub(q).reshape(b, h, dil, nb, B_BLOCK, d)
    kb = band(to_sub(k))
    vb = band(to_sub(v))
    sc = jnp.einsum('bhrnqd,bhrnkd->bhrnqk', qb, kb).astype(jnp.float32) * scale
    a_idx = jnp.arange(B_BLOCK)
    c_idx = jnp.arange(3 * B_BLOCK)
    rel = c_idx[None, :] - B_BLOCK - a_idx[:, None]
    kidx = (jnp.arange(nb)[:, None, None] - 1) * B_BLOCK + c_idx[None, None, :]
    valid = (jnp.abs(rel) <= half)[None] & (kidx >= 0) & (kidx < L)
    dist = (dil * jnp.abs(rel)).astype(jnp.float32)
    bias = -slopes[:, None, None, None, None] * dist
    sc = jnp.where(valid, sc + bias, NEG)
    m = jnp.max(sc, axis=-1, keepdims=True)
    p = jnp.exp(sc - m)
    den = jnp.sum(p, axis=-1, keepdims=True)
    o = jnp.einsum('bhrnqk,bhrnkd->bhrnqd', (p / den).astype(v.dtype), vb)
    lse = (m + jnp.log(den))[..., 0]
    o = o.reshape(b, h, dil, lp, d)[:, :, :, :L].transpose(0, 1, 3, 2, 4).reshape(b, h, s_len, d)
    lse = lse.reshape(b, h, dil, lp)[:, :, :, :L].transpose(0, 1, 3, 2).reshape(b, h, s_len)
    return o, lse


def _dilated_attention(q, k, v, slopes):
    outs, lses = [], []
    for window, dil in B_CONFIGS:
        o, lse = _dilated_branch(q, k, v, window, dil, slopes)
        outs.append(o)
        lses.append(lse)
    w = jax.nn.softmax(jnp.stack(lses, axis=0), axis=0)
    o = jnp.sum(w[..., None] * jnp.stack(outs, axis=0).astype(jnp.float32), axis=0)
    return o.astype(q.dtype)


def _mla_attention(q_nope, q_rope, k_nope, k_rope, v):
    scale = (C_NOPE + C_ROPE) ** -0.5

    def block(i, qn, qr):
        s = (jnp.einsum('bhqd,bhkd->bhqk', qn, k_nope)
             + jnp.einsum('bhqr,bkr->bhqk', qr, k_rope)).astype(jnp.float32) * scale
        p = jax.nn.softmax(s, axis=-1)
        return jnp.einsum('bhqk,bhkd->bhqd', p.astype(v.dtype), v)

    return _sweep_queries(block, (q_nope, q_rope))


def setup_inputs(seed: int = 0) -> dict:
    key = jax.random.key(seed)
    ks = jax.random.split(key, 20)
    f32 = jnp.float32

    def nrm(k, shape, scale):
        return jax.random.normal(k, shape, f32) * scale

    def gain(k, shape):
        return 1.0 + 0.01 * jax.random.normal(k, shape, f32)

    return {
        "x": jax.random.normal(ks[0], (BATCH, SEQ, D_MODEL), f32),
        "w_in": nrm(ks[1], (DEPTH, D_MODEL, N_IN), D_MODEL ** -0.5),
        "g_attn": gain(ks[2], (DEPTH, D_MODEL)),
        "a_lq1": nrm(ks[3], (DEPTH, A_QK_DIM), 0.1),
        "a_lk1": nrm(ks[4], (DEPTH, A_QK_DIM), 0.1),
        "a_lq2": nrm(ks[5], (DEPTH, A_QK_DIM), 0.1),
        "a_lk2": nrm(ks[6], (DEPTH, A_QK_DIM), 0.1),
        "a_subln": gain(ks[7], (DEPTH, A_V_DIM)),
        "c_g_q": gain(ks[8], (DEPTH, C_Q_RANK)),
        "c_w_uq": nrm(ks[9], (DEPTH, C_Q_RANK, C_HEADS * (C_NOPE + C_ROPE)), C_Q_RANK ** -0.5),
        "c_g_kv": gain(ks[10], (DEPTH, C_KV_RANK)),
        "c_w_ukv": nrm(ks[11], (DEPTH, C_KV_RANK, C_HEADS * (C_NOPE + C_V)), C_KV_RANK ** -0.5),
        "w_out": nrm(ks[12], (DEPTH, D_MIX, D_MODEL), D_MIX ** -0.5),
        "g_ffn": gain(ks[13], (DEPTH, D_MODEL)),
        "w_up": nrm(ks[14], (DEPTH, D_MODEL, 2 * D_FF), D_MODEL ** -0.5),
        "conv_w": nrm(ks[15], (DEPTH, CONV_W, 2 * D_FF), CONV_W ** -0.5),
        "conv_b": nrm(ks[16], (DEPTH, 2 * D_FF), 0.01),
        "w_down": nrm(ks[17], (DEPTH, D_FF, D_MODEL), D_FF ** -0.5),
        "g_final": gain(ks[18], (D_MODEL,)),
    }


def reference(x, w_in, g_attn, a_lq1, a_lk1, a_lq2, a_lk2, a_subln, c_g_q, c_w_uq,
              c_g_kv, c_w_ukv, w_out, g_ffn, w_up, conv_w, conv_b, w_down, g_final):
    b, s_len, _ = x.shape
    slopes_a, slopes_b = _alibi_slopes()
    pos = jnp.arange(s_len, dtype=jnp.float32)
    inv_freq = ROPE_BASE ** (-jnp.arange(0, C_ROPE, 2, dtype=jnp.float32) / C_ROPE)
    ang = pos[:, None] * inv_freq[None, :]
    cos, sin = jnp.cos(ang), jnp.sin(ang)
    split_idx = [int(i) for i in np.cumsum(IN_SPLITS)[:-1]]

    for l in range(DEPTH):
        h = _rmsnorm(x, g_attn[l])
        proj = h @ w_in[l]
        a_q, a_k, a_v, b_q, b_k, b_v, c_q, c_kv, c_kr = jnp.split(proj, split_idx, axis=-1)

        a_q = a_q.reshape(b, s_len, A_HEADS, 2, A_QK_DIM).transpose(0, 2, 3, 1, 4)
        a_k = a_k.reshape(b, s_len, A_HEADS, 2, A_QK_DIM).transpose(0, 2, 3, 1, 4)
        lam_init = 0.8 - 0.6 * math.exp(-0.3 * l)
        lam = (jnp.exp(jnp.sum(a_lq1[l].astype(jnp.float32) * a_lk1[l].astype(jnp.float32)))
               - jnp.exp(jnp.sum(a_lq2[l].astype(jnp.float32) * a_lk2[l].astype(jnp.float32)))
               + lam_init)
        a_o = _diff_attention(a_q[:, :, 0], a_q[:, :, 1], a_k[:, :, 0], a_k[:, :, 1],
                              _heads(a_v, A_HEADS), lam, slopes_a)
        a_o = _rmsnorm(a_o, a_subln[l]) * (1.0 - lam_init)
        a_out = _merge_heads(a_o)

        b_o = _dilated_attention(_heads(b_q, B_HEADS), _heads(b_k, B_HEADS),
                                 _heads(b_v, B_HEADS), slopes_b)
        b_out = _merge_heads(b_o)

        cq = _rmsnorm(c_q, c_g_q[l]) @ c_w_uq[l]
        cq = cq.reshape(b, s_len, C_HEADS, C_NOPE + C_ROPE)
        q_nope = cq[..., :C_NOPE].transpose(0, 2, 1, 3)
        q_rope = _rope(cq[..., C_NOPE:], cos[:, None, :], sin[:, None, :]).transpose(0, 2, 1, 3)
        ckv = (_rmsnorm(c_kv, c_g_kv[l]) @ c_w_ukv[l]).reshape(b, s_len, C_HEADS, C_NOPE + C_V)
        k_nope = ckv[..., :C_NOPE].transpose(0, 2, 1, 3)
        c_v = ckv[..., C_NOPE:].transpose(0, 2, 1, 3)
        k_rope = _rope(c_kr, cos, sin)
        c_out = _merge_heads(_mla_attention(q_nope, q_rope, k_nope, k_rope, c_v))

        mix = jnp.concatenate([a_out, b_out, c_out], axis=-1)
        x = x + mix @ w_out[l]

        h = _rmsnorm(x, g_ffn[l])
        u = h @ w_up[l]
        up = jnp.pad(u, ((0, 0), (1, 1), (0, 0)))
        cw = conv_w[l]
        u = up[:, :-2] * cw[0] + up[:, 1:-1] * cw[1] + up[:, 2:] * cw[2] + conv_b[l]
        gate, val = u[..., :D_FF], u[..., D_FF:]
        x = x + (jax.nn.silu(gate) * val) @ w_down[l]

    return _rmsnorm(x, g_final)
```

```python
import functools
import math

import numpy as np
import jax
import jax.numpy as jnp
from jax import lax
from jax.experimental import pallas as pl
from jax.experimental.pallas import tpu as pltpu

A_HEADS, A_QK_DIM, A_V_DIM = 4, 32, 64
B_HEADS, B_DIM = 4, 64
B_DILATIONS = (1, 4, 16)
B_HALF = 64
C_HEADS, C_NOPE, C_ROPE, C_V = 8, 64, 32, 64
C_Q_RANK, C_KV_RANK = 256, 128
C_HEAD_PAD = 128
ROPE_BASE = 10000.0
CONV_W = 3
EPS = 1e-6
NEG = -1e30

_N_ALIBI = A_HEADS + B_HEADS
_SLOPES = [2.0 ** (-8.0 * i / _N_ALIBI) for i in range(1, _N_ALIBI + 1)]
SLOPES_A = tuple(_SLOPES[0::2])
SLOPES_B = tuple(_SLOPES[1::2])

PROJ_ROWS = 512
ATTN_Q_ROWS = 256
BAND_Q_ROWS = 128
FFN_ROWS = 1024
FFN_COLS = 256
HALO_ROWS = 16
VMEM_LIMIT = 56 * 1024 * 1024

F32 = jnp.float32
BF16 = jnp.bfloat16


def _cparams(sem):
    return pltpu.CompilerParams(dimension_semantics=sem, vmem_limit_bytes=VMEM_LIMIT)


def _rms(x, g):
    return x * lax.rsqrt(jnp.mean(x * x, axis=-1, keepdims=True) + EPS) * g


def _dot(a, b):
    return jnp.dot(a, b, preferred_element_type=F32)


def _dot_nt(a, b):
    return lax.dot_general(a, b, (((1,), (1,)), ((), ())), preferred_element_type=F32)


def _lane_mask(width, lo, hi):
    lane = lax.broadcasted_iota(jnp.int32, (1, width), 1)
    return (lane >= lo) & (lane < hi)


def _proj_kernel(x_ref, g_ref, wa_ref, wb_ref, wc_ref, gq_ref, gkv_ref, w1_ref, w2_ref,
                 wk_ref, wv_ref, cosq_ref, sinq_ref, cosk_ref, sink_ref,
                 qkva_ref, qkvb_ref, qc_ref, kc_ref, vc_ref):
    h = _rms(x_ref[...], g_ref[...]).astype(BF16)
    qkva_ref[...] = _dot(h, wa_ref[...]).astype(BF16)
    qkvb_ref[...] = _dot(h, wb_ref[...])
    pc = _dot(h, wc_ref[...])
    hq = _rms(pc[:, 0:C_Q_RANK], gq_ref[...]).astype(BF16)
    hkv = _rms(pc[:, C_Q_RANK:C_Q_RANK + C_KV_RANK], gkv_ref[...]).astype(BF16)
    kr = pc[:, 384:512]
    krs = pc[:, 512:640]
    q = _dot(hq, w1_ref[...]) * cosq_ref[...] + _dot(hq, w2_ref[...]) * sinq_ref[...]
    qc_ref[...] = q.astype(BF16)
    k_rope = kr * cosk_ref[...] + krs * sink_ref[...]
    kn = _dot(hkv, wk_ref[...])
    for hd in range(C_HEADS):
        sl = slice(hd * C_HEAD_PAD, (hd + 1) * C_HEAD_PAD)
        kc_ref[:, sl] = (kn[:, sl] + k_rope).astype(BF16)
    vc_ref[...] = _dot(hkv, wv_ref[...]).astype(BF16)


def _proj_call(x2, g, wa, wb, wc, gq, gkv, w1, w2, wk, wv, cosq, sinq, cosk, sink, seq):
    t, d = x2.shape
    rows = PROJ_ROWS
    nseq = seq // rows
    full = lambda a: pl.BlockSpec(a.shape, lambda i: (0, 0))
    tab = lambda a: pl.BlockSpec((rows, a.shape[1]), lambda i: (i % nseq, 0))
    row = lambda w: pl.BlockSpec((rows, w), lambda i: (i, 0))
    return pl.pallas_call(
        _proj_kernel,
        grid=(t // rows,),
        in_specs=[row(d), full(g), full(wa), full(wb), full(wc), full(gq), full(gkv),
                  full(w1), full(w2), full(wk), full(wv), tab(cosq), tab(sinq), tab(cosk), tab(sink)],
        out_specs=[row(768), row(768), row(1024), row(1024), row(512)],
        out_shape=[jax.ShapeDtypeStruct((t, 768), BF16), jax.ShapeDtypeStruct((t, 768), F32),
                   jax.ShapeDtypeStruct((t, 1024), BF16), jax.ShapeDtypeStruct((t, 1024), BF16),
                   jax.ShapeDtypeStruct((t, 512), BF16)],
        compiler_params=_cparams(("parallel",)),
        name="proj",
    )(x2, g, wa, wb, wc, gq, gkv, w1, w2, wk, wv, cosq, sinq, cosk, sink)


def _diff_attn_kernel(lq1_ref, lk1_ref, lq2_ref, lk2_ref, gsub_ref, q_ref, k_ref, v_ref, o_ref,
                      *, lam_init, seq):
    tq = q_ref.shape[0]
    width = q_ref.shape[1]
    i = pl.program_id(1)
    lam = (jnp.exp(jnp.sum(lq1_ref[...] * lk1_ref[...], axis=-1, keepdims=True))
           - jnp.exp(jnp.sum(lq2_ref[...] * lk2_ref[...], axis=-1, keepdims=True)) + lam_init)
    qpos = i * tq + lax.broadcasted_iota(jnp.int32, (tq, 1), 0)
    kpos = lax.broadcasted_iota(jnp.int32, (1, seq), 1)
    dist = jnp.abs(qpos - kpos).astype(F32)
    scale = A_QK_DIM ** -0.5
    q = q_ref[...]
    k = k_ref[...]
    v = v_ref[...]
    gsub = gsub_ref[...]
    acc = jnp.zeros((tq, width), F32)
    for hd in range(A_HEADS):
        base = hd * 2 * A_QK_DIM
        bias = dist * (-SLOPES_A[hd])

        def probs(lo):
            qm = jnp.where(_lane_mask(width, lo, lo + A_QK_DIM), q, jnp.zeros_like(q))
            s = _dot_nt(qm, k) * scale + bias
            e = jnp.exp(s - jnp.max(s, axis=-1, keepdims=True))
            return e, jnp.sum(e, axis=-1, keepdims=True)

        e1, l1 = probs(base)
        e2, l2 = probs(base + A_QK_DIM)
        p = e1 * (1.0 / l1) - e2 * (lam / l2)
        o = _dot(p.astype(BF16), v)
        mv = _lane_mask(width, hd * A_V_DIM, (hd + 1) * A_V_DIM)
        ms = jnp.sum(jnp.where(mv, o * o, 0.0), axis=-1, keepdims=True) * (1.0 / A_V_DIM)
        o = o * lax.rsqrt(ms + EPS) * gsub * (1.0 - lam_init)
        acc = jnp.where(mv, o, acc)
    o_ref[...] = acc.astype(BF16)


def _diff_attn_call(qkva, lq1, lk1, lq2, lk2, gsub, lam_init, batch, seq):
    tq = ATTN_Q_ROWS
    nq = seq // tq
    w = A_HEADS * A_V_DIM
    small = lambda a: pl.BlockSpec(a.shape, lambda b, i: (0, 0))
    return pl.pallas_call(
        functools.partial(_diff_attn_kernel, lam_init=lam_init, seq=seq),
        grid=(batch, nq),
        in_specs=[small(lq1), small(lk1), small(lq2), small(lk2), small(gsub),
                  pl.BlockSpec((tq, w), lambda b, i: (b * nq + i, 0)),
                  pl.BlockSpec((seq, w), lambda b, i: (b, 1)),
                  pl.BlockSpec((seq, w), lambda b, i: (b, 2))],
        out_specs=pl.BlockSpec((tq, w), lambda b, i: (b * nq + i, 0)),
        out_shape=jax.ShapeDtypeStruct((batch * seq, w), BF16),
        compiler_params=_cparams(("parallel", "parallel")),
        name="diff_attn",
    )(lq1, lk1, lq2, lk2, gsub, qkva, qkva, qkva)


def _mla_kernel(q_ref, k_ref, v_ref, o_ref):
    tq = q_ref.shape[0]
    pair = 2 * C_V
    for pr in range(C_HEADS // 2):
        vpair = v_ref[:, pr * pair:(pr + 1) * pair]
        acc = jnp.zeros((tq, pair), F32)
        for sub in range(2):
            hd = 2 * pr + sub
            sl = slice(hd * C_HEAD_PAD, (hd + 1) * C_HEAD_PAD)
            s = _dot_nt(q_ref[:, sl], k_ref[:, sl])
            e = jnp.exp(s - jnp.max(s, axis=-1, keepdims=True))
            l = jnp.sum(e, axis=-1, keepdims=True)
            o = _dot(e.astype(BF16), vpair) * (1.0 / l)
            acc = jnp.where(_lane_mask(pair, sub * C_V, (sub + 1) * C_V), o, acc)
        o_ref[:, pr * pair:(pr + 1) * pair] = acc.astype(BF16)


def _mla_call(qc, kc, vc, batch, seq):
    tq = ATTN_Q_ROWS
    nq = seq // tq
    return pl.pallas_call(
        _mla_kernel,
        grid=(batch, nq),
        in_specs=[pl.BlockSpec((tq, qc.shape[1]), lambda b, i: (b * nq + i, 0)),
                  pl.BlockSpec((seq, kc.shape[1]), lambda b, i: (b, 0)),
                  pl.BlockSpec((seq, vc.shape[1]), lambda b, i: (b, 0))],
        out_specs=pl.BlockSpec((tq, vc.shape[1]), lambda b, i: (b * nq + i, 0)),
        out_shape=jax.ShapeDtypeStruct((batch * seq, vc.shape[1]), BF16),
        compiler_params=_cparams(("parallel", "parallel")),
        name="mla_attn",
    )(qc, kc, vc)


def _dilated_kernel(q0_ref, q1_ref, k0_ref, k1_ref, v0_ref, v1_ref, o_ref,
                    qs, ks, vs, osub, lsub, onat, lnat, *, seq):
    half_w = q0_ref.shape[1]
    width = 2 * half_w
    tq = BAND_Q_ROWS
    nblk = seq // tq
    scale = B_DIM ** -0.5

    def band_pass(dil):
        sub_len = seq // dil
        nqb = sub_len // tq
        win = min(2 * tq, sub_len)

        def body(t, carry):
            qb = t & (nqb - 1)
            kw = jnp.clip(qb * tq - B_HALF, 0, sub_len - win)
            krow = pl.multiple_of((t - qb) * tq + kw, B_HALF)
            qrow = pl.multiple_of(t * tq, tq)
            q = qs[pl.ds(qrow, tq), :]
            k = ks[pl.ds(krow, win), :]
            v = vs[pl.ds(krow, win), :]
            qpos = qb * tq + lax.broadcasted_iota(jnp.int32, (tq, 1), 0)
            kpos = kw + lax.broadcasted_iota(jnp.int32, (1, win), 1)
            rel = jnp.abs(qpos - kpos)
            valid = rel <= B_HALF
            relf = rel.astype(F32)
            o_acc = jnp.zeros((tq, width), F32)
            l_acc = jnp.zeros((tq, width), F32)
            for hd in range(B_HEADS):
                mh = _lane_mask(width, hd * B_DIM, (hd + 1) * B_DIM)
                qm = jnp.where(mh, q, jnp.zeros_like(q))
                sc = _dot_nt(qm, k) * scale + relf * (-SLOPES_B[hd] * dil)
                sc = jnp.where(valid, sc, NEG)
                m = jnp.max(sc, axis=-1, keepdims=True)
                p = jnp.exp(sc - m)
                den = jnp.sum(p, axis=-1, keepdims=True)
                o = _dot(p.astype(BF16), v) * (1.0 / den)
                o_acc = jnp.where(mh, o, o_acc)
                l_acc = jnp.where(mh, m + jnp.log(den), l_acc)
            osub[pl.ds(qrow, tq), :] = o_acc
            lsub[pl.ds(qrow, tq), :] = l_acc
            return carry

        lax.fori_loop(0, nblk, body, 0)

    for bi, dil in enumerate(B_DILATIONS):
        sub_len = seq // dil
        for r in range(dil):
            rows = pl.ds(r, sub_len, stride=dil) if dil > 1 else pl.ds(0, seq)
            dst = slice(r * sub_len, (r + 1) * sub_len)
            for hf, (qr, kr, vr) in enumerate(((q0_ref, k0_ref, v0_ref), (q1_ref, k1_ref, v1_ref))):
                lanes = slice(hf * half_w, (hf + 1) * half_w)
                qs[dst, lanes] = qr[rows, :].astype(BF16)
                ks[dst, lanes] = kr[rows, :].astype(BF16)
                vs[dst, lanes] = vr[rows, :].astype(BF16)
        band_pass(dil)
        for r in range(dil):
            rows = pl.ds(r, sub_len, stride=dil) if dil > 1 else pl.ds(0, seq)
            src = slice(r * sub_len, (r + 1) * sub_len)
            for hf in range(2):
                lanes = slice(hf * half_w, (hf + 1) * half_w)
                onat[bi, hf, rows, :] = osub[src, lanes]
                lnat[bi, hf, rows, :] = lsub[src, lanes]

    chunk = 256

    def mix(c, carry):
        rows = pl.ds(pl.multiple_of(c * chunk, chunk), chunk)
        for hf in range(2):
            l0, l1, l2 = lnat[0, hf, rows, :], lnat[1, hf, rows, :], lnat[2, hf, rows, :]
            mx = jnp.maximum(jnp.maximum(l0, l1), l2)
            w0, w1, w2 = jnp.exp(l0 - mx), jnp.exp(l1 - mx), jnp.exp(l2 - mx)
            num = w0 * onat[0, hf, rows, :] + w1 * onat[1, hf, rows, :] + w2 * onat[2, hf, rows, :]
            o_ref[rows, hf * half_w:(hf + 1) * half_w] = (num / (w0 + w1 + w2)).astype(BF16)
        return carry

    lax.fori_loop(0, seq // chunk, mix, 0)


def _dilated_call(qkvb, batch, seq):
    w = B_HEADS * B_DIM
    hw = w // 2
    return pl.pallas_call(
        functools.partial(_dilated_kernel, seq=seq),
        grid=(batch,),
        in_specs=[pl.BlockSpec((seq, hw), functools.partial(lambda b, c: (b, c), c=c)) for c in range(6)],
        out_specs=pl.BlockSpec((seq, w), lambda b: (b, 0)),
        out_shape=jax.ShapeDtypeStruct((batch * seq, w), BF16),
        scratch_shapes=[pltpu.VMEM((seq, w), BF16), pltpu.VMEM((seq, w), BF16), pltpu.VMEM((seq, w), BF16),
                        pltpu.VMEM((seq, w), F32), pltpu.VMEM((seq, w), F32),
                        pltpu.VMEM((len(B_DILATIONS), 2, seq, hw), F32),
                        pltpu.VMEM((len(B_DILATIONS), 2, seq, hw), F32)],
        compiler_params=_cparams(("parallel",)),
        name="dilated_attn",
    )(*([qkvb] * 6))


def _out_kernel(a_ref, b_ref, c_ref, x_ref, woa_ref, wob_ref, woc_ref, g_ref, x1_ref, h2_ref):
    x1 = (x_ref[...] + _dot(a_ref[...], woa_ref[...]) + _dot(b_ref[...], wob_ref[...])
          + _dot(c_ref[...], woc_ref[...]))
    x1_ref[...] = x1
    h2_ref[...] = _rms(x1, g_ref[...]).astype(BF16)


def _out_call(a, b, c, x2, woa, wob, woc, g):
    t, d = x2.shape
    rows = PROJ_ROWS
    full = lambda arr: pl.BlockSpec(arr.shape, lambda i: (0, 0))
    row = lambda w: pl.BlockSpec((rows, w), lambda i: (i, 0))
    return pl.pallas_call(
        _out_kernel,
        grid=(t // rows,),
        in_specs=[row(a.shape[1]), row(b.shape[1]), row(c.shape[1]), row(d),
                  full(woa), full(wob), full(woc), full(g)],
        out_specs=[row(d), row(d)],
        out_shape=[jax.ShapeDtypeStruct((t, d), F32), jax.ShapeDtypeStruct((t, d), BF16)],
        compiler_params=_cparams(("parallel",)),
        name="out_proj",
    )(a, b, c, x2, woa, wob, woc, g)


def _ffn_kernel(h_ref, hp_ref, hn_ref, x1_ref, wg_ref, wv_ref, cwg_ref, cwv_ref, cbg_ref, cbv_ref,
                wd_ref, gf_ref, o_ref, *, tiles_per_seq, final_norm):
    rows = h_ref.shape[0]
    i = pl.program_id(0)
    j = pl.program_id(1)
    si = i % tiles_per_seq
    has_prev = si > 0
    has_next = si < tiles_per_seq - 1

    @pl.when(j == 0)
    def _():
        o_ref[...] = x1_ref[...]

    h = h_ref[...]
    ridx = lax.broadcasted_iota(jnp.int32, (rows, 1), 0)

    def conv_branch(w_ref, cw_ref, cb_ref):
        w = w_ref[...]
        u = _dot(h, w)
        u_before = _dot(hp_ref[...], w)[HALO_ROWS - 1:HALO_ROWS, :]
        u_after = _dot(hn_ref[...], w)[0:1, :]
        u_before = jnp.where(has_prev, u_before, 0.0)
        u_after = jnp.where(has_next, u_after, 0.0)
        u_dn = jnp.where(ridx == 0, u_before, pltpu.roll(u, 1, axis=0))
        u_up = jnp.where(ridx == rows - 1, u_after, pltpu.roll(u, rows - 1, axis=0))
        cw = cw_ref[...]
        return u_dn * cw[0:1, :] + u * cw[1:2, :] + u_up * cw[2:3, :] + cb_ref[...]

    gate = conv_branch(wg_ref, cwg_ref, cbg_ref)
    val = conv_branch(wv_ref, cwv_ref, cbv_ref)
    act = (gate * jax.nn.sigmoid(gate) * val).astype(BF16)
    o_ref[...] += _dot(act, wd_ref[...])

    if final_norm:
        @pl.when(j == pl.num_programs(1) - 1)
        def _():
            o_ref[...] = _rms(o_ref[...], gf_ref[...])


def _ffn_call(h2, x1, wup, cw, cb, wdown, gf, seq, final_norm):
    t, d = x1.shape
    rows, cols = FFN_ROWS, FFN_COLS
    dff = wdown.shape[0]
    nj = dff // cols
    tiles_per_seq = seq // rows
    halo_per_tile = rows // HALO_ROWS
    last_halo = t // HALO_ROWS - 1
    return pl.pallas_call(
        functools.partial(_ffn_kernel, tiles_per_seq=tiles_per_seq, final_norm=final_norm),
        grid=(t // rows, nj),
        in_specs=[pl.BlockSpec((rows, d), lambda i, j: (i, 0)),
                  pl.BlockSpec((HALO_ROWS, d), lambda i, j: (jnp.maximum(i * halo_per_tile - 1, 0), 0)),
                  pl.BlockSpec((HALO_ROWS, d), lambda i, j: (jnp.minimum((i + 1) * halo_per_tile, last_halo), 0)),
                  pl.BlockSpec((rows, d), lambda i, j: (i, 0)),
                  pl.BlockSpec((d, cols), lambda i, j: (0, j)),
                  pl.BlockSpec((d, cols), lambda i, j: (0, nj + j)),
                  pl.BlockSpec((CONV_W, cols), lambda i, j: (0, j)),
                  pl.BlockSpec((CONV_W, cols), lambda i, j: (0, nj + j)),
                  pl.BlockSpec((1, cols), lambda i, j: (0, j)),
                  pl.BlockSpec((1, cols), lambda i, j: (0, nj + j)),
                  pl.BlockSpec((cols, d), lambda i, j: (j, 0)),
                  pl.BlockSpec((1, d), lambda i, j: (0, 0))],
        out_specs=pl.BlockSpec((rows, d), lambda i, j: (i, 0)),
        out_shape=jax.ShapeDtypeStruct((t, d), F32),
        compiler_params=_cparams(("parallel", "arbitrary")),
        name="conv_ffn",
    )(h2, h2, h2, x1, wup, wup, cw, cw, cb, cb, wdown, gf)


def _rotate_half_cols(w):
    half = w.shape[1] // 2
    return jnp.concatenate([-w[:, half:], w[:, :half]], axis=1)


def _prep_layer(w_in, c_w_uq, c_w_ukv, w_out):
    d = w_in.shape[0]
    wa = w_in[:, 0:768]
    wb = w_in[:, 768:1536]
    c_q = w_in[:, 1536:1536 + C_Q_RANK]
    c_kv = w_in[:, 1792:1792 + C_KV_RANK]
    c_kr = w_in[:, 1920:1920 + C_ROPE]
    z = lambda n: jnp.zeros((d, n), w_in.dtype)
    wc = jnp.concatenate([c_q, c_kv, z(64), c_kr, z(32), z(64), _rotate_half_cols(c_kr), z(32)], axis=1)
    uq = c_w_uq.reshape(C_Q_RANK, C_HEADS, C_NOPE + C_ROPE)
    zq = lambda n: jnp.zeros((C_Q_RANK, C_HEADS, n), uq.dtype)
    w1 = jnp.concatenate([uq, zq(32)], axis=2).reshape(C_Q_RANK, C_HEADS * C_HEAD_PAD)
    uq_rope = uq[:, :, C_NOPE:]
    uq_rot = jnp.concatenate([-uq_rope[:, :, C_ROPE // 2:], uq_rope[:, :, :C_ROPE // 2]], axis=2)
    w2 = jnp.concatenate([zq(C_NOPE), uq_rot, zq(32)], axis=2).reshape(C_Q_RANK, C_HEADS * C_HEAD_PAD)
    ukv = c_w_ukv.reshape(C_KV_RANK, C_HEADS, C_NOPE + C_V)
    zk = jnp.zeros((C_KV_RANK, C_HEADS, C_HEAD_PAD - C_NOPE), ukv.dtype)
    wk = jnp.concatenate([ukv[:, :, :C_NOPE], zk], axis=2).reshape(C_KV_RANK, C_HEADS * C_HEAD_PAD)
    wv = ukv[:, :, C_NOPE:].reshape(C_KV_RANK, C_HEADS * C_V)
    bf = lambda a: a.astype(BF16)
    return dict(wa=bf(wa), wb=bf(wb), wc=bf(wc), w1=bf(w1), w2=bf(w2), wk=bf(wk), wv=bf(wv),
                woa=bf(w_out[0:256]), wob=bf(w_out[256:512]), woc=bf(w_out[512:1024]))


def _rope_tables(seq):
    pos = jnp.arange(seq, dtype=F32)
    inv_freq = ROPE_BASE ** (-jnp.arange(0, C_ROPE, 2, dtype=F32) / C_ROPE)
    ang = pos[:, None] * inv_freq[None, :]
    cos, sin = jnp.cos(ang), jnp.sin(ang)
    cos2 = jnp.concatenate([cos, cos], axis=1)
    sin2 = jnp.concatenate([sin, sin], axis=1)
    scale = (C_NOPE + C_ROPE) ** -0.5
    ones = jnp.ones((seq, C_NOPE), F32)
    z = lambda n: jnp.zeros((seq, n), F32)
    cos_head = jnp.concatenate([ones, cos2, z(32)], axis=1)
    sin_head = jnp.concatenate([z(C_NOPE), sin2, z(32)], axis=1)
    cosq = jnp.tile(cos_head, (1, C_HEADS)) * scale
    sinq = jnp.tile(sin_head, (1, C_HEADS)) * scale
    cosk = jnp.concatenate([z(C_NOPE), cos2, z(32)], axis=1)
    sink = sin_head
    return cosq, sinq, cosk, sink


def kernel(x, w_in, g_attn, a_lq1, a_lk1, a_lq2, a_lk2, a_subln, c_g_q, c_w_uq, c_g_kv, c_w_ukv,
           w_out, g_ffn, w_up, conv_w, conv_b, w_down, g_final):
    batch, seq, d = x.shape
    depth = w_in.shape[0]
    assert seq % (16 * BAND_Q_ROWS) == 0 and seq % FFN_ROWS == 0 and d == 1024
    cosq, sinq, cosk, sink = _rope_tables(seq)
    x2 = x.reshape(batch * seq, d)
    row = lambda v: v[None, :]
    for l in range(depth):
        p = _prep_layer(w_in[l], c_w_uq[l], c_w_ukv[l], w_out[l])
        lam_init = 0.8 - 0.6 * math.exp(-0.3 * l)
        qkva, qkvb, qc, kc, vc = _proj_call(
            x2, row(g_attn[l]), p["wa"], p["wb"], p["wc"], row(c_g_q[l]), row(c_g_kv[l]),
            p["w1"], p["w2"], p["wk"], p["wv"], cosq, sinq, cosk, sink, seq)
        a_out = _diff_attn_call(qkva, row(a_lq1[l]), row(a_lk1[l]), row(a_lq2[l]), row(a_lk2[l]),
                                row(jnp.tile(a_subln[l], A_HEADS)), lam_init, batch, seq)
        b_out = _dilated_call(qkvb, batch, seq)
        c_out = _mla_call(qc, kc, vc, batch, seq)
        x1, h2 = _out_call(a_out, b_out, c_out, x2, p["woa"], p["wob"], p["woc"], row(g_ffn[l]))
        x2 = _ffn_call(h2, x1, w_up[l].astype(BF16), conv_w[l], row(conv_b[l]), w_down[l].astype(BF16),
                       row(g_final), seq, final_norm=(l == depth - 1))
    return x2.reshape(batch, seq, d)
```

```python
import functools
import math

import numpy as np
import jax
import jax.numpy as jnp
from jax import lax
from jax.experimental import pallas as pl
from jax.experimental.pallas import tpu as pltpu

A_HEADS, A_QK_DIM, A_V_DIM = 4, 32, 64
B_HEADS, B_DIM = 4, 64
B_DILATIONS = (1, 4, 16)
B_HALF = 64
N_BAND_OFFSETS = 3
C_HEADS, C_NOPE, C_ROPE, C_V = 8, 64, 32, 64
C_Q_RANK, C_KV_RANK = 256, 128
C_HEAD_PAD = 128
ROPE_BASE = 10000.0
CONV_W = 3
EPS = 1e-6
NEG = -1e30

_N_ALIBI = A_HEADS + B_HEADS
_SLOPES = [2.0 ** (-8.0 * i / _N_ALIBI) for i in range(1, _N_ALIBI + 1)]
SLOPES_A = tuple(_SLOPES[0::2])
SLOPES_B = tuple(_SLOPES[1::2])

PROJ_ROWS = 512
ATTN_Q_ROWS = 256
BAND_Q_ROWS = 128
FFN_ROWS = 1024
FFN_COLS = 256
HALO_ROWS = 16
VMEM_LIMIT = 56 * 1024 * 1024

F32 = jnp.float32
BF16 = jnp.bfloat16


def _cparams(sem):
    return pltpu.CompilerParams(dimension_semantics=sem, vmem_limit_bytes=VMEM_LIMIT)


def _rms(x, g):
    return x * lax.rsqrt(jnp.mean(x * x, axis=-1, keepdims=True) + EPS) * g


def _dot(a, b):
    return jnp.dot(a, b, preferred_element_type=F32)


def _dot_nt(a, b):
    return lax.dot_general(a, b, (((1,), (1,)), ((), ())), preferred_element_type=F32)


def _lane_mask(width, lo, hi):
    lane = lax.broadcasted_iota(jnp.int32, (1, width), 1)
    return (lane >= lo) & (lane < hi)


def _proj_kernel(x_ref, g_ref, wa_ref, wb_ref, wc_ref, gq_ref, gkv_ref, w1_ref, w2_ref,
                 wk_ref, wv_ref, cosq_ref, sinq_ref, cosk_ref, sink_ref,
                 qkva_ref, qkvb_ref, qc_ref, kc_ref, vc_ref):
    h = _rms(x_ref[...], g_ref[...]).astype(BF16)
    qkva_ref[...] = _dot(h, wa_ref[...]).astype(BF16)
    qkvb_ref[...] = _dot(h, wb_ref[...])
    pc = _dot(h, wc_ref[...])
    hq = _rms(pc[:, 0:C_Q_RANK], gq_ref[...]).astype(BF16)
    hkv = _rms(pc[:, C_Q_RANK:C_Q_RANK + C_KV_RANK], gkv_ref[...]).astype(BF16)
    kr = pc[:, 384:512]
    krs = pc[:, 512:640]
    q = _dot(hq, w1_ref[...]) * cosq_ref[...] + _dot(hq, w2_ref[...]) * sinq_ref[...]
    qc_ref[...] = q.astype(BF16)
    k_rope = kr * cosk_ref[...] + krs * sink_ref[...]
    kn = _dot(hkv, wk_ref[...])
    for hd in range(C_HEADS):
        sl = slice(hd * C_HEAD_PAD, (hd + 1) * C_HEAD_PAD)
        kc_ref[:, sl] = (kn[:, sl] + k_rope).astype(BF16)
    vc_ref[...] = _dot(hkv, wv_ref[...]).astype(BF16)


def _proj_call(x2, g, wa, wb, wc, gq, gkv, w1, w2, wk, wv, cosq, sinq, cosk, sink, seq):
    t, d = x2.shape
    rows = PROJ_ROWS
    nseq = seq // rows
    full = lambda a: pl.BlockSpec(a.shape, lambda i: (0, 0))
    tab = lambda a: pl.BlockSpec((rows, a.shape[1]), lambda i: (i % nseq, 0))
    row = lambda w: pl.BlockSpec((rows, w), lambda i: (i, 0))
    return pl.pallas_call(
        _proj_kernel,
        grid=(t // rows,),
        in_specs=[row(d), full(g), full(wa), full(wb), full(wc), full(gq), full(gkv),
                  full(w1), full(w2), full(wk), full(wv), tab(cosq), tab(sinq), tab(cosk), tab(sink)],
        out_specs=[row(768), row(768), row(1024), row(1024), row(512)],
        out_shape=[jax.ShapeDtypeStruct((t, 768), BF16), jax.ShapeDtypeStruct((t, 768), F32),
                   jax.ShapeDtypeStruct((t, 1024), BF16), jax.ShapeDtypeStruct((t, 1024), BF16),
                   jax.ShapeDtypeStruct((t, 512), BF16)],
        compiler_params=_cparams(("parallel",)),
        name="proj",
    )(x2, g, wa, wb, wc, gq, gkv, w1, w2, wk, wv, cosq, sinq, cosk, sink)


def _diff_attn_kernel(lq1_ref, lk1_ref, lq2_ref, lk2_ref, gsub_ref, q_ref, k_ref, v_ref, o_ref,
                      *, lam_init, seq):
    tq = q_ref.shape[0]
    width = q_ref.shape[1]
    i = pl.program_id(1)
    lam = (jnp.exp(jnp.sum(lq1_ref[...] * lk1_ref[...], axis=-1, keepdims=True))
           - jnp.exp(jnp.sum(lq2_ref[...] * lk2_ref[...], axis=-1, keepdims=True)) + lam_init)
    qpos = i * tq + lax.broadcasted_iota(jnp.int32, (tq, 1), 0)
    kpos = lax.broadcasted_iota(jnp.int32, (1, seq), 1)
    dist = jnp.abs(qpos - kpos).astype(F32)
    scale = A_QK_DIM ** -0.5
    q = q_ref[...]
    k = k_ref[...]
    v = v_ref[...]
    gsub = gsub_ref[...]
    acc = jnp.zeros((tq, width), F32)
    for hd in range(A_HEADS):
        base = hd * 2 * A_QK_DIM
        bias = dist * (-SLOPES_A[hd])

        def probs(lo):
            qm = jnp.where(_lane_mask(width, lo, lo + A_QK_DIM), q, jnp.zeros_like(q))
            s = _dot_nt(qm, k) * scale + bias
            e = jnp.exp(s - jnp.max(s, axis=-1, keepdims=True))
            return e, jnp.sum(e, axis=-1, keepdims=True)

        e1, l1 = probs(base)
        e2, l2 = probs(base + A_QK_DIM)
        p = e1 * (1.0 / l1) - e2 * (lam / l2)
        o = _dot(p.astype(BF16), v)
        mv = _lane_mask(width, hd * A_V_DIM, (hd + 1) * A_V_DIM)
        ms = jnp.sum(jnp.where(mv, o * o, 0.0), axis=-1, keepdims=True) * (1.0 / A_V_DIM)
        o = o * lax.rsqrt(ms + EPS) * gsub * (1.0 - lam_init)
        acc = jnp.where(mv, o, acc)
    o_ref[...] = acc.astype(BF16)


def _diff_attn_call(qkva, lq1, lk1, lq2, lk2, gsub, lam_init, batch, seq):
    tq = ATTN_Q_ROWS
    nq = seq // tq
    w = A_HEADS * A_V_DIM
    small = lambda a: pl.BlockSpec(a.shape, lambda b, i: (0, 0))
    return pl.pallas_call(
        functools.partial(_diff_attn_kernel, lam_init=lam_init, seq=seq),
        grid=(batch, nq),
        in_specs=[small(lq1), small(lk1), small(lq2), small(lk2), small(gsub),
                  pl.BlockSpec((tq, w), lambda b, i: (b * nq + i, 0)),
                  pl.BlockSpec((seq, w), lambda b, i: (b, 1)),
                  pl.BlockSpec((seq, w), lambda b, i: (b, 2))],
        out_specs=pl.BlockSpec((tq, w), lambda b, i: (b * nq + i, 0)),
        out_shape=jax.ShapeDtypeStruct((batch * seq, w), BF16),
        compiler_params=_cparams(("parallel", "parallel")),
        name="diff_attn",
    )(lq1, lk1, lq2, lk2, gsub, qkva, qkva, qkva)


def _mla_kernel(q_ref, k_ref, v_ref, o_ref):
    tq = q_ref.shape[0]
    pair = 2 * C_V
    for pr in range(C_HEADS // 2):
        vpair = v_ref[:, pr * pair:(pr + 1) * pair]
        acc = jnp.zeros((tq, pair), F32)
        for sub in range(2):
            hd = 2 * pr + sub
            sl = slice(hd * C_HEAD_PAD, (hd + 1) * C_HEAD_PAD)
            s = _dot_nt(q_ref[:, sl], k_ref[:, sl])
            e = jnp.exp(s - jnp.max(s, axis=-1, keepdims=True))
            l = jnp.sum(e, axis=-1, keepdims=True)
            o = _dot(e.astype(BF16), vpair) * (1.0 / l)
            acc = jnp.where(_lane_mask(pair, sub * C_V, (sub + 1) * C_V), o, acc)
        o_ref[:, pr * pair:(pr + 1) * pair] = acc.astype(BF16)


def _mla_call(qc, kc, vc, batch, seq):
    tq = ATTN_Q_ROWS
    nq = seq // tq
    return pl.pallas_call(
        _mla_kernel,
        grid=(batch, nq),
        in_specs=[pl.BlockSpec((tq, qc.shape[1]), lambda b, i: (b * nq + i, 0)),
                  pl.BlockSpec((seq, kc.shape[1]), lambda b, i: (b, 0)),
                  pl.BlockSpec((seq, vc.shape[1]), lambda b, i: (b, 0))],
        out_specs=pl.BlockSpec((tq, vc.shape[1]), lambda b, i: (b * nq + i, 0)),
        out_shape=jax.ShapeDtypeStruct((batch * seq, vc.shape[1]), BF16),
        compiler_params=_cparams(("parallel", "parallel")),
        name="mla_attn",
    )(qc, kc, vc)


def _dilated_kernel(q0_ref, q1_ref, k0_ref, k1_ref, v0_ref, v1_ref, o_ref,
                    x4, qs, ks, vs, res_o, res_l, mrg_o, mrg_l, bias_tab, *, seq):
    in_refs = (q0_ref, q1_ref, k0_ref, k1_ref, v0_ref, v1_ref)
    operand = (qs, qs, ks, ks, vs, vs)
    half_w = q0_ref.shape[1]
    width = 2 * half_w
    tq = BAND_Q_ROWS
    nblk = seq // tq
    len4, len16 = seq // 4, seq // 16
    scale = B_DIM ** -0.5

    def put_operand(c6, dst_rows, val):
        lanes = slice((c6 % 2) * half_w, (c6 % 2 + 1) * half_w)
        if c6 < 2:
            val = val * scale
        operand[c6][dst_rows, lanes] = val.astype(BF16)

    def band_pass(dil):
        sub_len = seq // dil
        nqb = sub_len // tq
        win = min(2 * tq, sub_len)
        tab_base = B_DILATIONS.index(dil) * N_BAND_OFFSETS

        def body(t, carry):
            qb = t & (nqb - 1)
            kw = jnp.clip(qb * tq - B_HALF, 0, sub_len - win)
            krow = pl.multiple_of((t - qb) * tq + kw, B_HALF)
            qrow = pl.multiple_of(t * tq, tq)
            q = qs[pl.ds(qrow, tq), :]
            k = ks[pl.ds(krow, win), :]
            v = vs[pl.ds(krow, win), :]
            masks = [_lane_mask(width, hd * B_DIM, (hd + 1) * B_DIM) for hd in range(B_HEADS)]
            qstack = jnp.concatenate([jnp.where(mh, q, jnp.zeros_like(q)) for mh in masks], axis=0)
            offset_idx = lax.shift_right_logical(qb * tq - kw, B_HALF.bit_length() - 1)
            sc = _dot_nt(qstack, k) + bias_tab[tab_base + offset_idx, :, 0:win]
            m = jnp.max(sc, axis=-1, keepdims=True)
            p = jnp.exp(sc - m)
            den = jnp.sum(p, axis=-1, keepdims=True)
            o = _dot(p.astype(BF16), v) * (1.0 / den)
            lse = m + jnp.log(den)
            o_acc = jnp.zeros((tq, width), F32)
            l_acc = jnp.zeros((tq, width), F32)
            for hd in range(B_HEADS):
                rows_h = slice(hd * tq, (hd + 1) * tq)
                o_acc = jnp.where(masks[hd], o[rows_h, :], o_acc)
                l_acc = jnp.where(masks[hd], lse[rows_h, :], l_acc)
            for hf in range(2):
                lanes = slice(hf * half_w, (hf + 1) * half_w)
                res_o[hf, pl.ds(qrow, tq), :] = o_acc[:, lanes]
                res_l[hf, pl.ds(qrow, tq), :] = l_acc[:, lanes]
            return carry

        lax.fori_loop(0, nblk, body, 0, unroll=2)

    @pl.when(pl.program_id(0) == 0)
    def _():
        row = lax.broadcasted_iota(jnp.int32, (B_HEADS * tq, 1), 0)
        col = lax.broadcasted_iota(jnp.int32, (1, 2 * tq), 1)
        for di, dil in enumerate(B_DILATIONS):
            slope_col = jnp.zeros((B_HEADS * tq, 1), F32)
            for hd in range(B_HEADS):
                slope_col = jnp.where(row // tq == hd, -SLOPES_B[hd] * dil, slope_col)
            for oi in range(N_BAND_OFFSETS):
                rel = jnp.abs(oi * B_HALF + (row & (tq - 1)) - col)
                bias_tab[di * N_BAND_OFFSETS + oi, :, :] = jnp.where(
                    rel <= B_HALF, rel.astype(F32) * slope_col, NEG)

    def merge_into(dst_o, dst_l, write_out):
        chunk = 256

        def body(c, carry):
            rows = pl.ds(pl.multiple_of(c * chunk, chunk), chunk)
            for hf in range(2):
                la, lb = res_l[hf, rows, :], dst_l[hf, rows, :]
                mx = jnp.maximum(la, lb)
                wa, wb = jnp.exp(la - mx), jnp.exp(lb - mx)
                tot = wa + wb
                o = (wa * res_o[hf, rows, :] + wb * dst_o[hf, rows, :]) / tot
                if write_out:
                    o_ref[rows, hf * half_w:(hf + 1) * half_w] = o.astype(BF16)
                else:
                    dst_o[hf, rows, :] = o
                    dst_l[hf, rows, :] = mx + jnp.log(tot)
            return carry

        lax.fori_loop(0, seq // chunk, body, 0)

    for c6 in range(6):
        for r4 in range(4):
            x4[c6, r4 * len4:(r4 + 1) * len4, :] = in_refs[c6][pl.ds(r4, len4, stride=4), :]

    for c6 in range(6):
        for r16 in range(16):
            r4, c = r16 % 4, r16 // 4
            put_operand(c6, slice(r16 * len16, (r16 + 1) * len16),
                        x4[c6, pl.ds(r4 * len4 + c, len16, stride=4), :])
    band_pass(16)
    for r16 in range(16):
        r4, c = r16 % 4, r16 // 4
        for hf in range(2):
            src = slice(r16 * len16, (r16 + 1) * len16)
            mrg_o[hf, pl.ds(r4 * len4 + c, len16, stride=4), :] = res_o[hf, src, :]
            mrg_l[hf, pl.ds(r4 * len4 + c, len16, stride=4), :] = res_l[hf, src, :]

    for c6 in range(6):
        put_operand(c6, slice(0, seq), x4[c6, :, :])
    band_pass(4)
    merge_into(mrg_o, mrg_l, write_out=False)
    for r4 in range(4):
        for hf in range(2):
            src = slice(r4 * len4, (r4 + 1) * len4)
            x4[hf, pl.ds(r4, len4, stride=4), :] = mrg_o[hf, src, :]
            x4[2 + hf, pl.ds(r4, len4, stride=4), :] = mrg_l[hf, src, :]

    for c6 in range(6):
        put_operand(c6, slice(0, seq), in_refs[c6][...])
    band_pass(1)
    merge_into(x4.at[0:2], x4.at[2:4], write_out=True)


def _dilated_call(qkvb, batch, seq):
    w = B_HEADS * B_DIM
    hw = w // 2
    return pl.pallas_call(
        functools.partial(_dilated_kernel, seq=seq),
        grid=(batch,),
        in_specs=[pl.BlockSpec((seq, hw), functools.partial(lambda b, c: (b, c), c=c)) for c in range(6)],
        out_specs=pl.BlockSpec((seq, w), lambda b: (b, 0)),
        out_shape=jax.ShapeDtypeStruct((batch * seq, w), BF16),
        scratch_shapes=[pltpu.VMEM((6, seq, hw), F32),
                        pltpu.VMEM((seq, w), BF16), pltpu.VMEM((seq, w), BF16), pltpu.VMEM((seq, w), BF16),
                        pltpu.VMEM((2, seq, hw), F32), pltpu.VMEM((2, seq, hw), F32),
                        pltpu.VMEM((2, seq, hw), F32), pltpu.VMEM((2, seq, hw), F32),
                        pltpu.VMEM((len(B_DILATIONS) * N_BAND_OFFSETS, B_HEADS * BAND_Q_ROWS, 2 * BAND_Q_ROWS), F32)],
        compiler_params=_cparams(("arbitrary",)),
        name="dilated_attn",
    )(*([qkvb] * 6))


def _out_kernel(a_ref, b_ref, c_ref, x_ref, woa_ref, wob_ref, woc_ref, g_ref, x1_ref, h2_ref):
    x1 = (x_ref[...] + _dot(a_ref[...], woa_ref[...]) + _dot(b_ref[...], wob_ref[...])
          + _dot(c_ref[...], woc_ref[...]))
    x1_ref[...] = x1
    h2_ref[...] = _rms(x1, g_ref[...]).astype(BF16)


def _out_call(a, b, c, x2, woa, wob, woc, g):
    t, d = x2.shape
    rows = PROJ_ROWS
    full = lambda arr: pl.BlockSpec(arr.shape, lambda i: (0, 0))
    row = lambda w: pl.BlockSpec((rows, w), lambda i: (i, 0))
    return pl.pallas_call(
        _out_kernel,
        grid=(t // rows,),
        in_specs=[row(a.shape[1]), row(b.shape[1]), row(c.shape[1]), row(d),
                  full(woa), full(wob), full(woc), full(g)],
        out_specs=[row(d), row(d)],
        out_shape=[jax.ShapeDtypeStruct((t, d), F32), jax.ShapeDtypeStruct((t, d), BF16)],
        compiler_params=_cparams(("parallel",)),
        name="out_proj",
    )(a, b, c, x2, woa, wob, woc, g)


def _ffn_kernel(h_ref, hp_ref, hn_ref, x1_ref, wg_ref, wv_ref, cwg_ref, cwv_ref, cbg_ref, cbv_ref,
                wd_ref, gf_ref, o_ref, *, tiles_per_seq, final_norm):
    rows = h_ref.shape[0]
    i = pl.program_id(0)
    j = pl.program_id(1)
    si = i % tiles_per_seq
    has_prev = si > 0
    has_next = si < tiles_per_seq - 1

    @pl.when(j == 0)
    def _():
        o_ref[...] = x1_ref[...]

    h = jnp.concatenate([h_ref[...], hp_ref[...], hn_ref[...]], axis=0)
    ridx = lax.broadcasted_iota(jnp.int32, (rows, 1), 0)

    def conv_branch(w_ref, cw_ref, cb_ref):
        u_all = _dot(h, w_ref[...])
        u = u_all[0:rows, :]
        u_before = u_all[rows + HALO_ROWS - 1:rows + HALO_ROWS, :]
        u_after = u_all[rows + HALO_ROWS:rows + HALO_ROWS + 1, :]
        u_before = jnp.where(has_prev, u_before, 0.0)
        u_after = jnp.where(has_next, u_after, 0.0)
        u_dn = jnp.where(ridx == 0, u_before, pltpu.roll(u, 1, axis=0))
        u_up = jnp.where(ridx == rows - 1, u_after, pltpu.roll(u, rows - 1, axis=0))
        cw = cw_ref[...]
        return u_dn * cw[0:1, :] + u * cw[1:2, :] + u_up * cw[2:3, :] + cb_ref[...]

    gate = conv_branch(wg_ref, cwg_ref, cbg_ref)
    val = conv_branch(wv_ref, cwv_ref, cbv_ref)
    act = (gate * jax.nn.sigmoid(gate) * val).astype(BF16)
    o_ref[...] += _dot(act, wd_ref[...])

    if final_norm:
        @pl.when(j == pl.num_programs(1) - 1)
        def _():
            o_ref[...] = _rms(o_ref[...], gf_ref[...])


def _ffn_call(h2, x1, wup, cw, cb, wdown, gf, seq, final_norm):
    t, d = x1.shape
    rows, cols = FFN_ROWS, FFN_COLS
    dff = wdown.shape[0]
    nj = dff // cols
    tiles_per_seq = seq // rows
    halo_per_tile = rows // HALO_ROWS
    last_halo = t // HALO_ROWS - 1
    return pl.pallas_call(
        functools.partial(_ffn_kernel, tiles_per_seq=tiles_per_seq, final_norm=final_norm),
        grid=(t // rows, nj),
        in_specs=[pl.BlockSpec((rows, d), lambda i, j: (i, 0)),
                  pl.BlockSpec((HALO_ROWS, d), lambda i, j: (jnp.maximum(i * halo_per_tile - 1, 0), 0)),
                  pl.BlockSpec((HALO_ROWS, d), lambda i, j: (jnp.minimum((i + 1) * halo_per_tile, last_halo), 0)),
                  pl.BlockSpec((rows, d), lambda i, j: (i, 0)),
                  pl.BlockSpec((d, cols), lambda i, j: (0, j)),
                  pl.BlockSpec((d, cols), lambda i, j: (0, nj + j)),
                  pl.BlockSpec((CONV_W, cols), lambda i, j: (0, j)),
                  pl.BlockSpec((CONV_W, cols), lambda i, j: (0, nj + j)),
                  pl.BlockSpec((1, cols), lambda i, j: (0, j)),
                  pl.BlockSpec((1, cols), lambda i, j: (0, nj + j)),
                  pl.BlockSpec((cols, d), lambda i, j: (j, 0)),
                  pl.BlockSpec((1, d), lambda i, j: (0, 0))],
        out_specs=pl.BlockSpec((rows, d), lambda i, j: (i, 0)),
        out_shape=jax.ShapeDtypeStruct((t, d), F32),
        compiler_params=_cparams(("parallel", "arbitrary")),
        name="conv_ffn",
    )(h2, h2, h2, x1, wup, wup, cw, cw, cb, cb, wdown, gf)


def _rotate_half_cols(w):
    half = w.shape[1] // 2
    return jnp.concatenate([-w[:, half:], w[:, :half]], axis=1)


def _prep_layer(w_in, c_w_uq, c_w_ukv, w_out):
    d = w_in.shape[0]
    wa = w_in[:, 0:768]
    wb = w_in[:, 768:1536]
    c_q = w_in[:, 1536:1536 + C_Q_RANK]
    c_kv = w_in[:, 1792:1792 + C_KV_RANK]
    c_kr = w_in[:, 1920:1920 + C_ROPE]
    z = lambda n: jnp.zeros((d, n), w_in.dtype)
    wc = jnp.concatenate([c_q, c_kv, z(64), c_kr, z(32), z(64), _rotate_half_cols(c_kr), z(32)], axis=1)
    uq = c_w_uq.reshape(C_Q_RANK, C_HEADS, C_NOPE + C_ROPE)
    zq = lambda n: jnp.zeros((C_Q_RANK, C_HEADS, n), uq.dtype)
    w1 = jnp.concatenate([uq, zq(32)], axis=2).reshape(C_Q_RANK, C_HEADS * C_HEAD_PAD)
    uq_rope = uq[:, :, C_NOPE:]
    uq_rot = jnp.concatenate([-uq_rope[:, :, C_ROPE // 2:], uq_rope[:, :, :C_ROPE // 2]], axis=2)
    w2 = jnp.concatenate([zq(C_NOPE), uq_rot, zq(32)], axis=2).reshape(C_Q_RANK, C_HEADS * C_HEAD_PAD)
    ukv = c_w_ukv.reshape(C_KV_RANK, C_HEADS, C_NOPE + C_V)
    zk = jnp.zeros((C_KV_RANK, C_HEADS, C_HEAD_PAD - C_NOPE), ukv.dtype)
    wk = jnp.concatenate([ukv[:, :, :C_NOPE], zk], axis=2).reshape(C_KV_RANK, C_HEADS * C_HEAD_PAD)
    wv = ukv[:, :, C_NOPE:].reshape(C_KV_RANK, C_HEADS * C_V)
    bf = lambda a: a.astype(BF16)
    return dict(wa=bf(wa), wb=bf(wb), wc=bf(wc), w1=bf(w1), w2=bf(w2), wk=bf(wk), wv=bf(wv),
                woa=bf(w_out[0:256]), wob=bf(w_out[256:512]), woc=bf(w_out[512:1024]))


def _rope_tables(seq):
    pos = jnp.arange(seq, dtype=F32)
    inv_freq = ROPE_BASE ** (-jnp.arange(0, C_ROPE, 2, dtype=F32) / C_ROPE)
    ang = pos[:, None] * inv_freq[None, :]
    cos, sin = jnp.cos(ang), jnp.sin(ang)
    cos2 = jnp.concatenate([cos, cos], axis=1)
    sin2 = jnp.concatenate([sin, sin], axis=1)
    scale = (C_NOPE + C_ROPE) ** -0.5
    ones = jnp.ones((seq, C_NOPE), F32)
    z = lambda n: jnp.zeros((seq, n), F32)
    cos_head = jnp.concatenate([ones, cos2, z(32)], axis=1)
    sin_head = jnp.concatenate([z(C_NOPE), sin2, z(32)], axis=1)
    cosq = jnp.tile(cos_head, (1, C_HEADS)) * scale
    sinq = jnp.tile(sin_head, (1, C_HEADS)) * scale
    cosk = jnp.concatenate([z(C_NOPE), cos2, z(32)], axis=1)
    sink = sin_head
    return cosq, sinq, cosk, sink


def kernel(x, w_in, g_attn, a_lq1, a_lk1, a_lq2, a_lk2, a_subln, c_g_q, c_w_uq, c_g_kv, c_w_ukv,
           w_out, g_ffn, w_up, conv_w, conv_b, w_down, g_final):
    batch, seq, d = x.shape
    depth = w_in.shape[0]
    assert seq % (16 * BAND_Q_ROWS) == 0 and seq % FFN_ROWS == 0 and d == 1024
    cosq, sinq, cosk, sink = _rope_tables(seq)
    x2 = x.reshape(batch * seq, d)
    row = lambda v: v[None, :]
    for l in range(depth):
        p = _prep_layer(w_in[l], c_w_uq[l], c_w_ukv[l], w_out[l])
        lam_init = 0.8 - 0.6 * math.exp(-0.3 * l)
        qkva, qkvb, qc, kc, vc = _proj_call(
            x2, row(g_attn[l]), p["wa"], p["wb"], p["wc"], row(c_g_q[l]), row(c_g_kv[l]),
            p["w1"], p["w2"], p["wk"], p["wv"], cosq, sinq, cosk, sink, seq)
        a_out = _diff_attn_call(qkva, row(a_lq1[l]), row(a_lk1[l]), row(a_lq2[l]), row(a_lk2[l]),
                                row(jnp.tile(a_subln[l], A_HEADS)), lam_init, batch, seq)
        b_out = _dilated_call(qkvb, batch, seq)
        c_out = _mla_call(qc, kc, vc, batch, seq)
        x1, h2 = _out_call(a_out, b_out, c_out, x2, p["woa"], p["wob"], p["woc"], row(g_ffn[l]))
        x2 = _ffn_call(h2, x1, w_up[l].astype(BF16), conv_w[l], row(conv_b[l]), w_down[l].astype(BF16),
                       row(g_final), seq, final_norm=(l == depth - 1))
    return x2.reshape(batch, seq, d)
```

```python
import functools
import math

import numpy as np
import jax
import jax.numpy as jnp
from jax import lax
from jax.experimental import pallas as pl
from jax.experimental.pallas import tpu as pltpu

A_HEADS, A_QK_DIM, A_V_DIM = 4, 32, 64
B_HEADS, B_DIM = 4, 64
B_DILATIONS = (1, 4, 16)
B_HALF = 64
N_BAND_OFFSETS = 3
C_HEADS, C_NOPE, C_ROPE, C_V = 8, 64, 32, 64
C_Q_RANK, C_KV_RANK = 256, 128
C_HEAD_PAD = 128
ROPE_BASE = 10000.0
CONV_W = 3
EPS = 1e-6
NEG = -1e30
LOG2E = math.log2(math.e)

_N_ALIBI = A_HEADS + B_HEADS
_SLOPES = [2.0 ** (-8.0 * i / _N_ALIBI) for i in range(1, _N_ALIBI + 1)]
SLOPES_A = tuple(_SLOPES[0::2])
SLOPES_B = tuple(_SLOPES[1::2])

PROJ_ROWS = 512
ATTN_Q_ROWS = 512
BAND_Q_ROWS = 128
FFN_ROWS = 1024
FFN_COLS = 256
HALO_ROWS = 16
VMEM_LIMIT = 56 * 1024 * 1024

F32 = jnp.float32
BF16 = jnp.bfloat16


def _cparams(sem):
    return pltpu.CompilerParams(dimension_semantics=sem, vmem_limit_bytes=VMEM_LIMIT)


def _rms(x, g):
    return x * lax.rsqrt(jnp.mean(x * x, axis=-1, keepdims=True) + EPS) * g


def _dot(a, b):
    return jnp.dot(a, b, preferred_element_type=F32)


def _dot_nt(a, b):
    return lax.dot_general(a, b, (((1,), (1,)), ((), ())), preferred_element_type=F32)


def _lane_mask(width, lo, hi):
    lane = lax.broadcasted_iota(jnp.int32, (1, width), 1)
    return (lane >= lo) & (lane < hi)


def _proj_kernel(x_ref, g_ref, wa_ref, wb_ref, wc_ref, gq_ref, gkv_ref, w1_ref, w2_ref,
                 wk_ref, wv_ref, cosq_ref, sinq_ref, cosk_ref, sink_ref,
                 qkva_ref, qkvb_ref, qc_ref, kc_ref, vc_ref):
    h = _rms(x_ref[...], g_ref[...]).astype(BF16)
    pa = _dot(h, wa_ref[...])
    qw = A_HEADS * 2 * A_QK_DIM
    qkva_ref[:, 0:qw] = (pa[:, 0:qw] * (A_QK_DIM ** -0.5 * LOG2E)).astype(BF16)
    qkva_ref[:, qw:] = pa[:, qw:].astype(BF16)
    qkvb_ref[...] = _dot(h, wb_ref[...])
    pc = _dot(h, wc_ref[...])
    hq = _rms(pc[:, 0:C_Q_RANK], gq_ref[...]).astype(BF16)
    hkv = _rms(pc[:, C_Q_RANK:C_Q_RANK + C_KV_RANK], gkv_ref[...]).astype(BF16)
    kr = pc[:, 384:512]
    krs = pc[:, 512:640]
    q = _dot(hq, w1_ref[...]) * cosq_ref[...] + _dot(hq, w2_ref[...]) * sinq_ref[...]
    qc_ref[...] = q.astype(BF16)
    k_rope = kr * cosk_ref[...] + krs * sink_ref[...]
    kn = _dot(hkv, wk_ref[...])
    for hd in range(C_HEADS):
        sl = slice(hd * C_HEAD_PAD, (hd + 1) * C_HEAD_PAD)
        kc_ref[:, sl] = (kn[:, sl] + k_rope).astype(BF16)
    vc_ref[...] = _dot(hkv, wv_ref[...]).astype(BF16)


def _proj_call(x2, g, wa, wb, wc, gq, gkv, w1, w2, wk, wv, cosq, sinq, cosk, sink, seq):
    t, d = x2.shape
    rows = PROJ_ROWS
    nseq = seq // rows
    full = lambda a: pl.BlockSpec(a.shape, lambda i: (0, 0))
    tab = lambda a: pl.BlockSpec((rows, a.shape[1]), lambda i: (i % nseq, 0))
    row = lambda w: pl.BlockSpec((rows, w), lambda i: (i, 0))
    return pl.pallas_call(
        _proj_kernel,
        grid=(t // rows,),
        in_specs=[row(d), full(g), full(wa), full(wb), full(wc), full(gq), full(gkv),
                  full(w1), full(w2), full(wk), full(wv), tab(cosq), tab(sinq), tab(cosk), tab(sink)],
        out_specs=[row(768), row(768), row(1024), row(1024), row(512)],
        out_shape=[jax.ShapeDtypeStruct((t, 768), BF16), jax.ShapeDtypeStruct((t, 768), F32),
                   jax.ShapeDtypeStruct((t, 1024), BF16), jax.ShapeDtypeStruct((t, 1024), BF16),
                   jax.ShapeDtypeStruct((t, 512), BF16)],
        compiler_params=_cparams(("parallel",)),
        name="proj",
    )(x2, g, wa, wb, wc, gq, gkv, w1, w2, wk, wv, cosq, sinq, cosk, sink)


def _diff_attn_kernel(lq1_ref, lk1_ref, lq2_ref, lk2_ref, gsub_ref, q_ref, k_ref, v_ref, o_ref,
                      *, lam_init, seq):
    tq = q_ref.shape[0]
    width = q_ref.shape[1]
    i = pl.program_id(1)
    lam = (jnp.exp(jnp.sum(lq1_ref[...] * lk1_ref[...], axis=-1, keepdims=True))
           - jnp.exp(jnp.sum(lq2_ref[...] * lk2_ref[...], axis=-1, keepdims=True)) + lam_init)
    qpos = i * tq + lax.broadcasted_iota(jnp.int32, (tq, 1), 0)
    kpos = lax.broadcasted_iota(jnp.int32, (1, seq), 1)
    dist = jnp.abs(qpos - kpos).astype(F32)
    q = q_ref[...]
    k = k_ref[...]
    v = v_ref[...]
    gsub = gsub_ref[...]
    acc = jnp.zeros((tq, width), F32)
    for hd in range(A_HEADS):
        base = hd * 2 * A_QK_DIM
        bias = dist * (-SLOPES_A[hd] * LOG2E)
        q12 = jnp.concatenate(
            [jnp.where(_lane_mask(width, lo, lo + A_QK_DIM), q, jnp.zeros_like(q))
             for lo in (base, base + A_QK_DIM)], axis=0)
        s12 = _dot_nt(q12, k)

        def probs(s):
            s = s + bias
            e = jnp.exp2(s - jnp.max(s, axis=-1, keepdims=True))
            return e.astype(BF16), jnp.sum(e, axis=-1, keepdims=True)

        e1, l1 = probs(s12[0:tq, :])
        e2, l2 = probs(s12[tq:2 * tq, :])
        o12 = _dot(jnp.concatenate([e1, e2], axis=0), v)
        o = o12[0:tq, :] * (1.0 / l1) - o12[tq:2 * tq, :] * (lam / l2)
        mv = _lane_mask(width, hd * A_V_DIM, (hd + 1) * A_V_DIM)
        ms = jnp.sum(jnp.where(mv, o * o, 0.0), axis=-1, keepdims=True) * (1.0 / A_V_DIM)
        o = o * lax.rsqrt(ms + EPS) * gsub * (1.0 - lam_init)
        acc = jnp.where(mv, o, acc)
    o_ref[...] = acc.astype(BF16)


def _diff_attn_call(qkva, lq1, lk1, lq2, lk2, gsub, lam_init, batch, seq):
    tq = ATTN_Q_ROWS
    nq = seq // tq
    w = A_HEADS * A_V_DIM
    small = lambda a: pl.BlockSpec(a.shape, lambda b, i: (0, 0))
    return pl.pallas_call(
        functools.partial(_diff_attn_kernel, lam_init=lam_init, seq=seq),
        grid=(batch, nq),
        in_specs=[small(lq1), small(lk1), small(lq2), small(lk2), small(gsub),
                  pl.BlockSpec((tq, w), lambda b, i: (b * nq + i, 0)),
                  pl.BlockSpec((seq, w), lambda b, i: (b, 1)),
                  pl.BlockSpec((seq, w), lambda b, i: (b, 2))],
        out_specs=pl.BlockSpec((tq, w), lambda b, i: (b * nq + i, 0)),
        out_shape=jax.ShapeDtypeStruct((batch * seq, w), BF16),
        compiler_params=_cparams(("parallel", "parallel")),
        name="diff_attn",
    )(lq1, lk1, lq2, lk2, gsub, qkva, qkva, qkva)


def _mla_kernel(q_ref, k_ref, v_ref, o_ref):
    tq = q_ref.shape[0]
    pair = 2 * C_V
    for pr in range(C_HEADS // 2):
        vpair = v_ref[:, pr * pair:(pr + 1) * pair]
        acc = jnp.zeros((tq, pair), F32)
        for sub in range(2):
            hd = 2 * pr + sub
            sl = slice(hd * C_HEAD_PAD, (hd + 1) * C_HEAD_PAD)
            s = _dot_nt(q_ref[:, sl], k_ref[:, sl])
            e = jnp.exp2(s - jnp.max(s, axis=-1, keepdims=True))
            l = jnp.sum(e, axis=-1, keepdims=True)
            o = _dot(e.astype(BF16), vpair) * (1.0 / l)
            acc = jnp.where(_lane_mask(pair, sub * C_V, (sub + 1) * C_V), o, acc)
        o_ref[:, pr * pair:(pr + 1) * pair] = acc.astype(BF16)


def _mla_call(qc, kc, vc, batch, seq):
    tq = ATTN_Q_ROWS
    nq = seq // tq
    return pl.pallas_call(
        _mla_kernel,
        grid=(batch, nq),
        in_specs=[pl.BlockSpec((tq, qc.shape[1]), lambda b, i: (b * nq + i, 0)),
                  pl.BlockSpec((seq, kc.shape[1]), lambda b, i: (b, 0)),
                  pl.BlockSpec((seq, vc.shape[1]), lambda b, i: (b, 0))],
        out_specs=pl.BlockSpec((tq, vc.shape[1]), lambda b, i: (b * nq + i, 0)),
        out_shape=jax.ShapeDtypeStruct((batch * seq, vc.shape[1]), BF16),
        compiler_params=_cparams(("parallel", "parallel")),
        name="mla_attn",
    )(qc, kc, vc)


def _dilated_kernel(q0_ref, q1_ref, k0_ref, k1_ref, v0_ref, v1_ref, o_ref,
                    x4, qs, ks, vs, res_o, res_l, mrg_o, mrg_l, bias_tab, *, seq):
    in_refs = (q0_ref, q1_ref, k0_ref, k1_ref, v0_ref, v1_ref)
    operand = (qs, qs, ks, ks, vs, vs)
    half_w = q0_ref.shape[1]
    width = 2 * half_w
    tq = BAND_Q_ROWS
    nblk = seq // tq
    len4, len16 = seq // 4, seq // 16
    scale = B_DIM ** -0.5

    def put_operand(c6, dst_rows, val):
        lanes = slice((c6 % 2) * half_w, (c6 % 2 + 1) * half_w)
        if c6 < 2:
            val = val * scale
        operand[c6][dst_rows, lanes] = val.astype(BF16)

    def band_pass(dil):
        sub_len = seq // dil
        nqb = sub_len // tq
        win = min(2 * tq, sub_len)
        tab_base = B_DILATIONS.index(dil) * N_BAND_OFFSETS

        def body(t, carry):
            qb = t & (nqb - 1)
            kw = jnp.clip(qb * tq - B_HALF, 0, sub_len - win)
            krow = pl.multiple_of((t - qb) * tq + kw, B_HALF)
            qrow = pl.multiple_of(t * tq, tq)
            q = qs[pl.ds(qrow, tq), :]
            k = ks[pl.ds(krow, win), :]
            v = vs[pl.ds(krow, win), :]
            masks = [_lane_mask(width, hd * B_DIM, (hd + 1) * B_DIM) for hd in range(B_HEADS)]
            qstack = jnp.concatenate([jnp.where(mh, q, jnp.zeros_like(q)) for mh in masks], axis=0)
            offset_idx = lax.shift_right_logical(qb * tq - kw, B_HALF.bit_length() - 1)
            sc = _dot_nt(qstack, k) + bias_tab[tab_base + offset_idx, :, 0:win]
            m = jnp.max(sc, axis=-1, keepdims=True)
            p = jnp.exp(sc - m)
            den = jnp.sum(p, axis=-1, keepdims=True)
            o = _dot(p.astype(BF16), v) * (1.0 / den)
            lse = m + jnp.log(den)
            o_acc = jnp.zeros((tq, width), F32)
            l_acc = jnp.zeros((tq, width), F32)
            for hd in range(B_HEADS):
                rows_h = slice(hd * tq, (hd + 1) * tq)
                o_acc = jnp.where(masks[hd], o[rows_h, :], o_acc)
                l_acc = jnp.where(masks[hd], lse[rows_h, :], l_acc)
            for hf in range(2):
                lanes = slice(hf * half_w, (hf + 1) * half_w)
                res_o[hf, pl.ds(qrow, tq), :] = o_acc[:, lanes]
                res_l[hf, pl.ds(qrow, tq), :] = l_acc[:, lanes]
            return carry

        lax.fori_loop(0, nblk, body, 0, unroll=2)

    @pl.when(pl.program_id(0) == 0)
    def _():
        row = lax.broadcasted_iota(jnp.int32, (B_HEADS * tq, 1), 0)
        col = lax.broadcasted_iota(jnp.int32, (1, 2 * tq), 1)
        for di, dil in enumerate(B_DILATIONS):
            slope_col = jnp.zeros((B_HEADS * tq, 1), F32)
            for hd in range(B_HEADS):
                slope_col = jnp.where(row // tq == hd, -SLOPES_B[hd] * dil, slope_col)
            for oi in range(N_BAND_OFFSETS):
                rel = jnp.abs(oi * B_HALF + (row & (tq - 1)) - col)
                bias_tab[di * N_BAND_OFFSETS + oi, :, :] = jnp.where(
                    rel <= B_HALF, rel.astype(F32) * slope_col, NEG)

    def merge_into(dst_o, dst_l, write_out):
        chunk = 256

        def body(c, carry):
            rows = pl.ds(pl.multiple_of(c * chunk, chunk), chunk)
            for hf in range(2):
                la, lb = res_l[hf, rows, :], dst_l[hf, rows, :]
                mx = jnp.maximum(la, lb)
                wa, wb = jnp.exp(la - mx), jnp.exp(lb - mx)
                tot = wa + wb
                o = (wa * res_o[hf, rows, :] + wb * dst_o[hf, rows, :]) / tot
                if write_out:
                    o_ref[rows, hf * half_w:(hf + 1) * half_w] = o.astype(BF16)
                else:
                    dst_o[hf, rows, :] = o
                    dst_l[hf, rows, :] = mx + jnp.log(tot)
            return carry

        lax.fori_loop(0, seq // chunk, body, 0)

    for c6 in range(6):
        for r4 in range(4):
            x4[c6, r4 * len4:(r4 + 1) * len4, :] = in_refs[c6][pl.ds(r4, len4, stride=4), :]

    for c6 in range(6):
        for r16 in range(16):
            r4, c = r16 % 4, r16 // 4
            put_operand(c6, slice(r16 * len16, (r16 + 1) * len16),
                        x4[c6, pl.ds(r4 * len4 + c, len16, stride=4), :])
    band_pass(16)
    for r16 in range(16):
        r4, c = r16 % 4, r16 // 4
        for hf in range(2):
            src = slice(r16 * len16, (r16 + 1) * len16)
            mrg_o[hf, pl.ds(r4 * len4 + c, len16, stride=4), :] = res_o[hf, src, :]
            mrg_l[hf, pl.ds(r4 * len4 + c, len16, stride=4), :] = res_l[hf, src, :]

    for c6 in range(6):
        put_operand(c6, slice(0, seq), x4[c6, :, :])
    band_pass(4)
    merge_into(mrg_o, mrg_l, write_out=False)
    for r4 in range(4):
        for hf in range(2):
            src = slice(r4 * len4, (r4 + 1) * len4)
            x4[hf, pl.ds(r4, len4, stride=4), :] = mrg_o[hf, src, :]
            x4[2 + hf, pl.ds(r4, len4, stride=4), :] = mrg_l[hf, src, :]

    for c6 in range(6):
        put_operand(c6, slice(0, seq), in_refs[c6][...])
    band_pass(1)
    merge_into(x4.at[0:2], x4.at[2:4], write_out=True)


def _dilated_call(qkvb, batch, seq):
    w = B_HEADS * B_DIM
    hw = w // 2
    return pl.pallas_call(
        functools.partial(_dilated_kernel, seq=seq),
        grid=(batch,),
        in_specs=[pl.BlockSpec((seq, hw), functools.partial(lambda b, c: (b, c), c=c)) for c in range(6)],
        out_specs=pl.BlockSpec((seq, w), lambda b: (b, 0)),
        out_shape=jax.ShapeDtypeStruct((batch * seq, w), BF16),
        scratch_shapes=[pltpu.VMEM((6, seq, hw), F32),
                        pltpu.VMEM((seq, w), BF16), pltpu.VMEM((seq, w), BF16), pltpu.VMEM((seq, w), BF16),
                        pltpu.VMEM((2, seq, hw), F32), pltpu.VMEM((2, seq, hw), F32),
                        pltpu.VMEM((2, seq, hw), F32), pltpu.VMEM((2, seq, hw), F32),
                        pltpu.VMEM((len(B_DILATIONS) * N_BAND_OFFSETS, B_HEADS * BAND_Q_ROWS, 2 * BAND_Q_ROWS), F32)],
        compiler_params=_cparams(("arbitrary",)),
        name="dilated_attn",
    )(*([qkvb] * 6))


def _out_kernel(a_ref, b_ref, c_ref, x_ref, woa_ref, wob_ref, woc_ref, g_ref, x1_ref, h2_ref):
    x1 = (x_ref[...] + _dot(a_ref[...], woa_ref[...]) + _dot(b_ref[...], wob_ref[...])
          + _dot(c_ref[...], woc_ref[...]))
    x1_ref[...] = x1
    h2_ref[...] = _rms(x1, g_ref[...]).astype(BF16)


def _out_call(a, b, c, x2, woa, wob, woc, g):
    t, d = x2.shape
    rows = PROJ_ROWS
    full = lambda arr: pl.BlockSpec(arr.shape, lambda i: (0, 0))
    row = lambda w: pl.BlockSpec((rows, w), lambda i: (i, 0))
    return pl.pallas_call(
        _out_kernel,
        grid=(t // rows,),
        in_specs=[row(a.shape[1]), row(b.shape[1]), row(c.shape[1]), row(d),
                  full(woa), full(wob), full(woc), full(g)],
        out_specs=[row(d), row(d)],
        out_shape=[jax.ShapeDtypeStruct((t, d), F32), jax.ShapeDtypeStruct((t, d), BF16)],
        compiler_params=_cparams(("parallel",)),
        name="out_proj",
    )(a, b, c, x2, woa, wob, woc, g)


def _ffn_kernel(h_ref, hp_ref, hn_ref, x1_ref, wg_ref, wv_ref, cwg_ref, cwv_ref, cbg_ref, cbv_ref,
                wd_ref, gf_ref, o_ref, *, tiles_per_seq, final_norm):
    rows = h_ref.shape[0]
    i = pl.program_id(0)
    j = pl.program_id(1)
    si = i % tiles_per_seq
    has_prev = si > 0
    has_next = si < tiles_per_seq - 1

    @pl.when(j == 0)
    def _():
        o_ref[...] = x1_ref[...]

    h = jnp.concatenate([h_ref[...], hp_ref[...], hn_ref[...]], axis=0)
    ridx = lax.broadcasted_iota(jnp.int32, (rows, 1), 0)

    def conv_branch(w_ref, cw_ref, cb_ref):
        u_all = _dot(h, w_ref[...])
        u = u_all[0:rows, :]
        u_before = u_all[rows + HALO_ROWS - 1:rows + HALO_ROWS, :]
        u_after = u_all[rows + HALO_ROWS:rows + HALO_ROWS + 1, :]
        u_before = jnp.where(has_prev, u_before, 0.0)
        u_after = jnp.where(has_next, u_after, 0.0)
        u_dn = jnp.where(ridx == 0, u_before, pltpu.roll(u, 1, axis=0))
        u_up = jnp.where(ridx == rows - 1, u_after, pltpu.roll(u, rows - 1, axis=0))
        cw = cw_ref[...]
        return u_dn * cw[0:1, :] + u * cw[1:2, :] + u_up * cw[2:3, :] + cb_ref[...]

    gate = conv_branch(wg_ref, cwg_ref, cbg_ref)
    val = conv_branch(wv_ref, cwv_ref, cbv_ref)
    act = (gate * jax.nn.sigmoid(gate) * val).astype(BF16)
    o_ref[...] += _dot(act, wd_ref[...])

    if final_norm:
        @pl.when(j == pl.num_programs(1) - 1)
        def _():
            o_ref[...] = _rms(o_ref[...], gf_ref[...])


def _ffn_call(h2, x1, wup, cw, cb, wdown, gf, seq, final_norm):
    t, d = x1.shape
    rows, cols = FFN_ROWS, FFN_COLS
    dff = wdown.shape[0]
    nj = dff // cols
    tiles_per_seq = seq // rows
    halo_per_tile = rows // HALO_ROWS
    last_halo = t // HALO_ROWS - 1
    return pl.pallas_call(
        functools.partial(_ffn_kernel, tiles_per_seq=tiles_per_seq, final_norm=final_norm),
        grid=(t // rows, nj),
        in_specs=[pl.BlockSpec((rows, d), lambda i, j: (i, 0)),
                  pl.BlockSpec((HALO_ROWS, d), lambda i, j: (jnp.maximum(i * halo_per_tile - 1, 0), 0)),
                  pl.BlockSpec((HALO_ROWS, d), lambda i, j: (jnp.minimum((i + 1) * halo_per_tile, last_halo), 0)),
                  pl.BlockSpec((rows, d), lambda i, j: (i, 0)),
                  pl.BlockSpec((d, cols), lambda i, j: (0, j)),
                  pl.BlockSpec((d, cols), lambda i, j: (0, nj + j)),
                  pl.BlockSpec((CONV_W, cols), lambda i, j: (0, j)),
                  pl.BlockSpec((CONV_W, cols), lambda i, j: (0, nj + j)),
                  pl.BlockSpec((1, cols), lambda i, j: (0, j)),
                  pl.BlockSpec((1, cols), lambda i, j: (0, nj + j)),
                  pl.BlockSpec((cols, d), lambda i, j: (j, 0)),
                  pl.BlockSpec((1, d), lambda i, j: (0, 0))],
        out_specs=pl.BlockSpec((rows, d), lambda i, j: (i, 0)),
        out_shape=jax.ShapeDtypeStruct((t, d), F32),
        compiler_params=_cparams(("parallel", "arbitrary")),
        name="conv_ffn",
    )(h2, h2, h2, x1, wup, wup, cw, cw, cb, cb, wdown, gf)


def _rotate_half_cols(w):
    half = w.shape[1] // 2
    return jnp.concatenate([-w[:, half:], w[:, :half]], axis=1)


def _prep_layer(w_in, c_w_uq, c_w_ukv, w_out):
    d = w_in.shape[0]
    wa = w_in[:, 0:768]
    wb = w_in[:, 768:1536]
    c_q = w_in[:, 1536:1536 + C_Q_RANK]
    c_kv = w_in[:, 1792:1792 + C_KV_RANK]
    c_kr = w_in[:, 1920:1920 + C_ROPE]
    z = lambda n: jnp.zeros((d, n), w_in.dtype)
    wc = jnp.concatenate([c_q, c_kv, z(64), c_kr, z(32), z(64), _rotate_half_cols(c_kr), z(32)], axis=1)
    uq = c_w_uq.reshape(C_Q_RANK, C_HEADS, C_NOPE + C_ROPE)
    zq = lambda n: jnp.zeros((C_Q_RANK, C_HEADS, n), uq.dtype)
    w1 = jnp.concatenate([uq, zq(32)], axis=2).reshape(C_Q_RANK, C_HEADS * C_HEAD_PAD)
    uq_rope = uq[:, :, C_NOPE:]
    uq_rot = jnp.concatenate([-uq_rope[:, :, C_ROPE // 2:], uq_rope[:, :, :C_ROPE // 2]], axis=2)
    w2 = jnp.concatenate([zq(C_NOPE), uq_rot, zq(32)], axis=2).reshape(C_Q_RANK, C_HEADS * C_HEAD_PAD)
    ukv = c_w_ukv.reshape(C_KV_RANK, C_HEADS, C_NOPE + C_V)
    zk = jnp.zeros((C_KV_RANK, C_HEADS, C_HEAD_PAD - C_NOPE), ukv.dtype)
    wk = jnp.concatenate([ukv[:, :, :C_NOPE], zk], axis=2).reshape(C_KV_RANK, C_HEADS * C_HEAD_PAD)
    wv = ukv[:, :, C_NOPE:].reshape(C_KV_RANK, C_HEADS * C_V)
    bf = lambda a: a.astype(BF16)
    return dict(wa=bf(wa), wb=bf(wb), wc=bf(wc), w1=bf(w1), w2=bf(w2), wk=bf(wk), wv=bf(wv),
                woa=bf(w_out[0:256]), wob=bf(w_out[256:512]), woc=bf(w_out[512:1024]))


def _rope_tables(seq):
    pos = jnp.arange(seq, dtype=F32)
    inv_freq = ROPE_BASE ** (-jnp.arange(0, C_ROPE, 2, dtype=F32) / C_ROPE)
    ang = pos[:, None] * inv_freq[None, :]
    cos, sin = jnp.cos(ang), jnp.sin(ang)
    cos2 = jnp.concatenate([cos, cos], axis=1)
    sin2 = jnp.concatenate([sin, sin], axis=1)
    scale = (C_NOPE + C_ROPE) ** -0.5 * LOG2E
    ones = jnp.ones((seq, C_NOPE), F32)
    z = lambda n: jnp.zeros((seq, n), F32)
    cos_head = jnp.concatenate([ones, cos2, z(32)], axis=1)
    sin_head = jnp.concatenate([z(C_NOPE), sin2, z(32)], axis=1)
    cosq = jnp.tile(cos_head, (1, C_HEADS)) * scale
    sinq = jnp.tile(sin_head, (1, C_HEADS)) * scale
    cosk = jnp.concatenate([z(C_NOPE), cos2, z(32)], axis=1)
    sink = sin_head
    return cosq, sinq, cosk, sink


def kernel(x, w_in, g_attn, a_lq1, a_lk1, a_lq2, a_lk2, a_subln, c_g_q, c_w_uq, c_g_kv, c_w_ukv,
           w_out, g_ffn, w_up, conv_w, conv_b, w_down, g_final):
    batch, seq, d = x.shape
    depth = w_in.shape[0]
    assert seq % (16 * BAND_Q_ROWS) == 0 and seq % FFN_ROWS == 0 and d == 1024
    cosq, sinq, cosk, sink = _rope_tables(seq)
    x2 = x.reshape(batch * seq, d)
    row = lambda v: v[None, :]
    for l in range(depth):
        p = _prep_layer(w_in[l], c_w_uq[l], c_w_ukv[l], w_out[l])
        lam_init = 0.8 - 0.6 * math.exp(-0.3 * l)
        qkva, qkvb, qc, kc, vc = _proj_call(
            x2, row(g_attn[l]), p["wa"], p["wb"], p["wc"], row(c_g_q[l]), row(c_g_kv[l]),
            p["w1"], p["w2"], p["wk"], p["wv"], cosq, sinq, cosk, sink, seq)
        a_out = _diff_attn_call(qkva, row(a_lq1[l]), row(a_lk1[l]), row(a_lq2[l]), row(a_lk2[l]),
                                row(jnp.tile(a_subln[l], A_HEADS)), lam_init, batch, seq)
        b_out = _dilated_call(qkvb, batch, seq)
        c_out = _mla_call(qc, kc, vc, batch, seq)
        x1, h2 = _out_call(a_out, b_out, c_out, x2, p["woa"], p["wob"], p["woc"], row(g_ffn[l]))
        x2 = _ffn_call(h2, x1, w_up[l].astype(BF16), conv_w[l], row(conv_b[l]), w_down[l].astype(BF16),
                       row(g_final), seq, final_norm=(l == depth - 1))
    return x2.reshape(batch, seq, d)
```

```python
import functools
import math

import numpy as np
import jax
import jax.numpy as jnp
from jax import lax
from jax.experimental import pallas as pl
from jax.experimental.pallas import tpu as pltpu

A_HEADS, A_QK_DIM, A_V_DIM = 4, 32, 64
B_HEADS, B_DIM = 4, 64
B_DILATIONS = (1, 4, 16)
B_HALF = 64
N_BAND_OFFSETS = 3
C_HEADS, C_NOPE, C_ROPE, C_V = 8, 64, 32, 64
C_Q_RANK, C_KV_RANK = 256, 128
C_HEAD_PAD = 128
ROPE_BASE = 10000.0
CONV_W = 3
EPS = 1e-6
NEG = -1e30
LOG2E = math.log2(math.e)

_N_ALIBI = A_HEADS + B_HEADS
_SLOPES = [2.0 ** (-8.0 * i / _N_ALIBI) for i in range(1, _N_ALIBI + 1)]
SLOPES_A = tuple(_SLOPES[0::2])
SLOPES_B = tuple(_SLOPES[1::2])

PROJ_ROWS = 512
ATTN_Q_ROWS = 512
BAND_Q_ROWS = 128
FFN_ROWS = 512
FFN_COLS = 256
HALO_ROWS = 16
VMEM_LIMIT = 56 * 1024 * 1024

F32 = jnp.float32
BF16 = jnp.bfloat16


def _cparams(sem):
    return pltpu.CompilerParams(dimension_semantics=sem, vmem_limit_bytes=VMEM_LIMIT)


def _rms(x, g):
    return x * lax.rsqrt(jnp.mean(x * x, axis=-1, keepdims=True) + EPS) * g


def _dot(a, b):
    return jnp.dot(a, b, preferred_element_type=F32)


def _dot_nt(a, b):
    return lax.dot_general(a, b, (((1,), (1,)), ((), ())), preferred_element_type=F32)


def _lane_mask(width, lo, hi):
    lane = lax.broadcasted_iota(jnp.int32, (1, width), 1)
    return (lane >= lo) & (lane < hi)


def _proj_kernel(x_ref, g_ref, wa_ref, wb_ref, wc_ref, gq_ref, gkv_ref, w1_ref, w2_ref,
                 wk_ref, wv_ref, cosq_ref, sinq_ref, cosk_ref, sink_ref,
                 qkva_ref, qkvb_ref, qc_ref, kc_ref, vc_ref):
    h = _rms(x_ref[...], g_ref[...]).astype(BF16)
    pa = _dot(h, wa_ref[...])
    qw = A_HEADS * 2 * A_QK_DIM
    qkva_ref[:, 0:qw] = (pa[:, 0:qw] * (A_QK_DIM ** -0.5 * LOG2E)).astype(BF16)
    qkva_ref[:, qw:] = pa[:, qw:].astype(BF16)
    qkvb_ref[...] = _dot(h, wb_ref[...])
    pc = _dot(h, wc_ref[...])
    hq = _rms(pc[:, 0:C_Q_RANK], gq_ref[...]).astype(BF16)
    hkv = _rms(pc[:, C_Q_RANK:C_Q_RANK + C_KV_RANK], gkv_ref[...]).astype(BF16)
    kr = pc[:, 384:512]
    krs = pc[:, 512:640]
    q = _dot(hq, w1_ref[...]) * cosq_ref[...] + _dot(hq, w2_ref[...]) * sinq_ref[...]
    qc_ref[...] = q.astype(BF16)
    k_rope = kr * cosk_ref[...] + krs * sink_ref[...]
    kn = _dot(hkv, wk_ref[...])
    for hd in range(C_HEADS):
        sl = slice(hd * C_HEAD_PAD, (hd + 1) * C_HEAD_PAD)
        kc_ref[:, sl] = (kn[:, sl] + k_rope).astype(BF16)
    vc_ref[...] = _dot(hkv, wv_ref[...]).astype(BF16)


def _proj_call(x2, g, wa, wb, wc, gq, gkv, w1, w2, wk, wv, cosq, sinq, cosk, sink, seq):
    t, d = x2.shape
    rows = PROJ_ROWS
    nseq = seq // rows
    full = lambda a: pl.BlockSpec(a.shape, lambda i: (0, 0))
    tab = lambda a: pl.BlockSpec((rows, a.shape[1]), lambda i: (i % nseq, 0))
    row = lambda w: pl.BlockSpec((rows, w), lambda i: (i, 0))
    return pl.pallas_call(
        _proj_kernel,
        grid=(t // rows,),
        in_specs=[row(d), full(g), full(wa), full(wb), full(wc), full(gq), full(gkv),
                  full(w1), full(w2), full(wk), full(wv), tab(cosq), tab(sinq), tab(cosk), tab(sink)],
        out_specs=[row(768), row(768), row(1024), row(1024), row(512)],
        out_shape=[jax.ShapeDtypeStruct((t, 768), BF16), jax.ShapeDtypeStruct((t, 768), F32),
                   jax.ShapeDtypeStruct((t, 1024), BF16), jax.ShapeDtypeStruct((t, 1024), BF16),
                   jax.ShapeDtypeStruct((t, 512), BF16)],
        compiler_params=_cparams(("parallel",)),
        name="proj",
    )(x2, g, wa, wb, wc, gq, gkv, w1, w2, wk, wv, cosq, sinq, cosk, sink)


def _diff_attn_kernel(lq1_ref, lk1_ref, lq2_ref, lk2_ref, gsub_ref, q_ref, k_ref, v_ref, o_ref,
                      *, lam_init, seq):
    tq = q_ref.shape[0]
    width = q_ref.shape[1]
    i = pl.program_id(1)
    lam = (jnp.exp(jnp.sum(lq1_ref[...] * lk1_ref[...], axis=-1, keepdims=True))
           - jnp.exp(jnp.sum(lq2_ref[...] * lk2_ref[...], axis=-1, keepdims=True)) + lam_init)
    qpos = i * tq + lax.broadcasted_iota(jnp.int32, (tq, 1), 0)
    kpos = lax.broadcasted_iota(jnp.int32, (1, seq), 1)
    dist = jnp.abs(qpos - kpos).astype(F32)
    q = q_ref[...]
    k = k_ref[...]
    v = v_ref[...]
    gsub = gsub_ref[...]
    acc = jnp.zeros((tq, width), F32)
    for hd in range(A_HEADS):
        base = hd * 2 * A_QK_DIM
        bias = dist * (-SLOPES_A[hd] * LOG2E)
        q12 = jnp.concatenate(
            [jnp.where(_lane_mask(width, lo, lo + A_QK_DIM), q, jnp.zeros_like(q))
             for lo in (base, base + A_QK_DIM)], axis=0)
        s12 = _dot_nt(q12, k)

        def probs(s):
            s = s + bias
            e = jnp.exp2(s - jnp.max(s, axis=-1, keepdims=True))
            return e.astype(BF16), jnp.sum(e, axis=-1, keepdims=True)

        e1, l1 = probs(s12[0:tq, :])
        e2, l2 = probs(s12[tq:2 * tq, :])
        o12 = _dot(jnp.concatenate([e1, e2], axis=0), v)
        o = o12[0:tq, :] * (1.0 / l1) - o12[tq:2 * tq, :] * (lam / l2)
        mv = _lane_mask(width, hd * A_V_DIM, (hd + 1) * A_V_DIM)
        ms = jnp.sum(jnp.where(mv, o * o, 0.0), axis=-1, keepdims=True) * (1.0 / A_V_DIM)
        o = o * lax.rsqrt(ms + EPS) * gsub * (1.0 - lam_init)
        acc = jnp.where(mv, o, acc)
    o_ref[...] = acc.astype(BF16)


def _diff_attn_call(qkva, lq1, lk1, lq2, lk2, gsub, lam_init, batch, seq):
    tq = ATTN_Q_ROWS
    nq = seq // tq
    w = A_HEADS * A_V_DIM
    small = lambda a: pl.BlockSpec(a.shape, lambda b, i: (0, 0))
    return pl.pallas_call(
        functools.partial(_diff_attn_kernel, lam_init=lam_init, seq=seq),
        grid=(batch, nq),
        in_specs=[small(lq1), small(lk1), small(lq2), small(lk2), small(gsub),
                  pl.BlockSpec((tq, w), lambda b, i: (b * nq + i, 0)),
                  pl.BlockSpec((seq, w), lambda b, i: (b, 1)),
                  pl.BlockSpec((seq, w), lambda b, i: (b, 2))],
        out_specs=pl.BlockSpec((tq, w), lambda b, i: (b * nq + i, 0)),
        out_shape=jax.ShapeDtypeStruct((batch * seq, w), BF16),
        compiler_params=_cparams(("parallel", "parallel")),
        name="diff_attn",
    )(lq1, lk1, lq2, lk2, gsub, qkva, qkva, qkva)


def _mla_kernel(q_ref, k_ref, v_ref, o_ref):
    tq = q_ref.shape[0]
    pair = 2 * C_V
    for pr in range(C_HEADS // 2):
        vpair = v_ref[:, pr * pair:(pr + 1) * pair]
        acc = jnp.zeros((tq, pair), F32)
        for sub in range(2):
            hd = 2 * pr + sub
            sl = slice(hd * C_HEAD_PAD, (hd + 1) * C_HEAD_PAD)
            s = _dot_nt(q_ref[:, sl], k_ref[:, sl])
            e = jnp.exp2(s - jnp.max(s, axis=-1, keepdims=True))
            l = jnp.sum(e, axis=-1, keepdims=True)
            o = _dot(e.astype(BF16), vpair) * (1.0 / l)
            acc = jnp.where(_lane_mask(pair, sub * C_V, (sub + 1) * C_V), o, acc)
        o_ref[:, pr * pair:(pr + 1) * pair] = acc.astype(BF16)


def _mla_call(qc, kc, vc, batch, seq):
    tq = ATTN_Q_ROWS
    nq = seq // tq
    return pl.pallas_call(
        _mla_kernel,
        grid=(batch, nq),
        in_specs=[pl.BlockSpec((tq, qc.shape[1]), lambda b, i: (b * nq + i, 0)),
                  pl.BlockSpec((seq, kc.shape[1]), lambda b, i: (b, 0)),
                  pl.BlockSpec((seq, vc.shape[1]), lambda b, i: (b, 0))],
        out_specs=pl.BlockSpec((tq, vc.shape[1]), lambda b, i: (b * nq + i, 0)),
        out_shape=jax.ShapeDtypeStruct((batch * seq, vc.shape[1]), BF16),
        compiler_params=_cparams(("parallel", "parallel")),
        name="mla_attn",
    )(qc, kc, vc)


def _dilated_kernel(q0_ref, q1_ref, k0_ref, k1_ref, v0_ref, v1_ref, o_ref,
                    x4, qs, ks, vs, res_o, res_l, mrg_o, mrg_l, bias_tab, *, seq):
    in_refs = (q0_ref, q1_ref, k0_ref, k1_ref, v0_ref, v1_ref)
    operand = (qs, qs, ks, ks, vs, vs)
    half_w = q0_ref.shape[1]
    width = 2 * half_w
    tq = BAND_Q_ROWS
    nblk = seq // tq
    len4, len16 = seq // 4, seq // 16
    scale = B_DIM ** -0.5

    def put_operand(c6, dst_rows, val):
        lanes = slice((c6 % 2) * half_w, (c6 % 2 + 1) * half_w)
        if c6 < 2:
            val = val * scale
        operand[c6][dst_rows, lanes] = val.astype(BF16)

    def band_pass(dil):
        sub_len = seq // dil
        nqb = sub_len // tq
        win = min(2 * tq, sub_len)
        tab_base = B_DILATIONS.index(dil) * N_BAND_OFFSETS

        def body(t, carry):
            qb = t & (nqb - 1)
            kw = jnp.clip(qb * tq - B_HALF, 0, sub_len - win)
            krow = pl.multiple_of((t - qb) * tq + kw, B_HALF)
            qrow = pl.multiple_of(t * tq, tq)
            q = qs[pl.ds(qrow, tq), :]
            k = ks[pl.ds(krow, win), :]
            v = vs[pl.ds(krow, win), :]
            masks = [_lane_mask(width, hd * B_DIM, (hd + 1) * B_DIM) for hd in range(B_HEADS)]
            qstack = jnp.concatenate([jnp.where(mh, q, jnp.zeros_like(q)) for mh in masks], axis=0)
            offset_idx = lax.shift_right_logical(qb * tq - kw, B_HALF.bit_length() - 1)
            sc = _dot_nt(qstack, k) + bias_tab[tab_base + offset_idx, :, 0:win]
            m = jnp.max(sc, axis=-1, keepdims=True)
            p = jnp.exp(sc - m)
            den = jnp.sum(p, axis=-1, keepdims=True)
            o = _dot(p.astype(BF16), v) * (1.0 / den)
            lse = m + jnp.log(den)
            o_acc = jnp.zeros((tq, width), F32)
            l_acc = jnp.zeros((tq, width), F32)
            for hd in range(B_HEADS):
                rows_h = slice(hd * tq, (hd + 1) * tq)
                o_acc = jnp.where(masks[hd], o[rows_h, :], o_acc)
                l_acc = jnp.where(masks[hd], lse[rows_h, :], l_acc)
            for hf in range(2):
                lanes = slice(hf * half_w, (hf + 1) * half_w)
                res_o[hf, pl.ds(qrow, tq), :] = o_acc[:, lanes]
                res_l[hf, pl.ds(qrow, tq), :] = l_acc[:, lanes]
            return carry

        lax.fori_loop(0, nblk, body, 0, unroll=2)

    @pl.when(pl.program_id(0) == 0)
    def _():
        row = lax.broadcasted_iota(jnp.int32, (B_HEADS * tq, 1), 0)
        col = lax.broadcasted_iota(jnp.int32, (1, 2 * tq), 1)
        for di, dil in enumerate(B_DILATIONS):
            slope_col = jnp.zeros((B_HEADS * tq, 1), F32)
            for hd in range(B_HEADS):
                slope_col = jnp.where(row // tq == hd, -SLOPES_B[hd] * dil, slope_col)
            for oi in range(N_BAND_OFFSETS):
                rel = jnp.abs(oi * B_HALF + (row & (tq - 1)) - col)
                bias_tab[di * N_BAND_OFFSETS + oi, :, :] = jnp.where(
                    rel <= B_HALF, rel.astype(F32) * slope_col, NEG)

    def merge_into(dst_o, dst_l, write_out):
        chunk = 256

        def body(c, carry):
            rows = pl.ds(pl.multiple_of(c * chunk, chunk), chunk)
            for hf in range(2):
                la, lb = res_l[hf, rows, :], dst_l[hf, rows, :]
                mx = jnp.maximum(la, lb)
                wa, wb = jnp.exp(la - mx), jnp.exp(lb - mx)
                tot = wa + wb
                o = (wa * res_o[hf, rows, :] + wb * dst_o[hf, rows, :]) / tot
                if write_out:
                    o_ref[rows, hf * half_w:(hf + 1) * half_w] = o.astype(BF16)
                else:
                    dst_o[hf, rows, :] = o
                    dst_l[hf, rows, :] = mx + jnp.log(tot)
            return carry

        lax.fori_loop(0, seq // chunk, body, 0)

    for c6 in range(6):
        for r4 in range(4):
            x4[c6, r4 * len4:(r4 + 1) * len4, :] = in_refs[c6][pl.ds(r4, len4, stride=4), :]

    for c6 in range(6):
        for r16 in range(16):
            r4, c = r16 % 4, r16 // 4
            put_operand(c6, slice(r16 * len16, (r16 + 1) * len16),
                        x4[c6, pl.ds(r4 * len4 + c, len16, stride=4), :])
    band_pass(16)
    for r16 in range(16):
        r4, c = r16 % 4, r16 // 4
        for hf in range(2):
            src = slice(r16 * len16, (r16 + 1) * len16)
            mrg_o[hf, pl.ds(r4 * len4 + c, len16, stride=4), :] = res_o[hf, src, :]
            mrg_l[hf, pl.ds(r4 * len4 + c, len16, stride=4), :] = res_l[hf, src, :]

    for c6 in range(6):
        put_operand(c6, slice(0, seq), x4[c6, :, :])
    band_pass(4)
    merge_into(mrg_o, mrg_l, write_out=False)
    for r4 in range(4):
        for hf in range(2):
            src = slice(r4 * len4, (r4 + 1) * len4)
            x4[hf, pl.ds(r4, len4, stride=4), :] = mrg_o[hf, src, :]
            x4[2 + hf, pl.ds(r4, len4, stride=4), :] = mrg_l[hf, src, :]

    for c6 in range(6):
        put_operand(c6, slice(0, seq), in_refs[c6][...])
    band_pass(1)
    merge_into(x4.at[0:2], x4.at[2:4], write_out=True)


def _dilated_call(qkvb, batch, seq):
    w = B_HEADS * B_DIM
    hw = w // 2
    return pl.pallas_call(
        functools.partial(_dilated_kernel, seq=seq),
        grid=(batch,),
        in_specs=[pl.BlockSpec((seq, hw), functools.partial(lambda b, c: (b, c), c=c)) for c in range(6)],
        out_specs=pl.BlockSpec((seq, w), lambda b: (b, 0)),
        out_shape=jax.ShapeDtypeStruct((batch * seq, w), BF16),
        scratch_shapes=[pltpu.VMEM((6, seq, hw), F32),
                        pltpu.VMEM((seq, w), BF16), pltpu.VMEM((seq, w), BF16), pltpu.VMEM((seq, w), BF16),
                        pltpu.VMEM((2, seq, hw), F32), pltpu.VMEM((2, seq, hw), F32),
                        pltpu.VMEM((2, seq, hw), F32), pltpu.VMEM((2, seq, hw), F32),
                        pltpu.VMEM((len(B_DILATIONS) * N_BAND_OFFSETS, B_HEADS * BAND_Q_ROWS, 2 * BAND_Q_ROWS), F32)],
        compiler_params=_cparams(("arbitrary",)),
        name="dilated_attn",
    )(*([qkvb] * 6))


def _out_kernel(a_ref, b_ref, c_ref, x_ref, woa_ref, wob_ref, woc_ref, g_ref, x1_ref, h2_ref):
    x1 = (x_ref[...] + _dot(a_ref[...], woa_ref[...]) + _dot(b_ref[...], wob_ref[...])
          + _dot(c_ref[...], woc_ref[...]))
    x1_ref[...] = x1
    h2_ref[...] = _rms(x1, g_ref[...]).astype(BF16)


def _out_call(a, b, c, x2, woa, wob, woc, g):
    t, d = x2.shape
    rows = PROJ_ROWS
    full = lambda arr: pl.BlockSpec(arr.shape, lambda i: (0, 0))
    row = lambda w: pl.BlockSpec((rows, w), lambda i: (i, 0))
    return pl.pallas_call(
        _out_kernel,
        grid=(t // rows,),
        in_specs=[row(a.shape[1]), row(b.shape[1]), row(c.shape[1]), row(d),
                  full(woa), full(wob), full(woc), full(g)],
        out_specs=[row(d), row(d)],
        out_shape=[jax.ShapeDtypeStruct((t, d), F32), jax.ShapeDtypeStruct((t, d), BF16)],
        compiler_params=_cparams(("parallel",)),
        name="out_proj",
    )(a, b, c, x2, woa, wob, woc, g)


def _ffn_kernel(h_ref, hp_ref, hn_ref, x1_ref, wup_ref, cwb_ref, wd_ref, gf_ref, o_ref,
                act_scr, ua_scr, ub_scr, *, tiles_per_seq, final_norm):
    rows = h_ref.shape[0]
    nchunk, _, cols2 = wup_ref.shape
    cols = cols2 // 2
    si = pl.program_id(0) % tiles_per_seq
    has_prev = si > 0
    has_next = si < tiles_per_seq - 1
    h = jnp.concatenate([h_ref[...], hp_ref[...], hn_ref[...]], axis=0)
    ridx = lax.broadcasted_iota(jnp.int32, (rows, 1), 0)

    def conv_gate(j, u_all):
        u = u_all[0:rows, :]
        u_before = u_all[rows + HALO_ROWS - 1:rows + HALO_ROWS, :]
        u_after = u_all[rows + HALO_ROWS:rows + HALO_ROWS + 1, :]
        u_before = jnp.where(has_prev, u_before, 0.0)
        u_after = jnp.where(has_next, u_after, 0.0)
        u_dn = jnp.where(ridx == 0, u_before, pltpu.roll(u, 1, axis=0))
        u_up = jnp.where(ridx == rows - 1, u_after, pltpu.roll(u, rows - 1, axis=0))
        cwb = cwb_ref[j]
        c = u_dn * cwb[0:1, :] + u * cwb[1:2, :] + u_up * cwb[2:3, :] + cwb[3:4, :]
        gate, val = c[:, 0:cols], c[:, cols:cols2]
        act_scr[j] = (gate * jax.nn.sigmoid(gate) * val).astype(BF16)

    assert nchunk % 2 == 1
    ua_scr[...] = _dot(h, wup_ref[0])

    def chunk_pair(jj, carry):
        j = 2 * jj
        ub_scr[...] = _dot(h, wup_ref[j + 1])
        conv_gate(j, ua_scr[...])
        ua_scr[...] = _dot(h, wup_ref[j + 2])
        conv_gate(j + 1, ub_scr[...])
        return carry

    lax.fori_loop(0, nchunk // 2, chunk_pair, 0)
    conv_gate(nchunk - 1, ua_scr[...])
    act =jnp.concatenate([act_scr[j] for j in range(nchunk)], axis=1)
    out = x1_ref[...] + _dot(act, wd_ref[...])
    if final_norm:
        out = _rms(out, gf_ref[...])
    o_ref[...] = out


def _ffn_call(h2, x1, wup_c, cwb_c, wdown, gf, seq, final_norm):
    t, d = x1.shape
    rows = FFN_ROWS
    nchunk, _, cols2 = wup_c.shape
    tiles_per_seq = seq // rows
    halo_per_tile = rows // HALO_ROWS
    last_halo = t // HALO_ROWS - 1
    resident = lambda a: pl.BlockSpec(a.shape, lambda i: (0,) * a.ndim, pipeline_mode=pl.Buffered(1))
    return pl.pallas_call(
        functools.partial(_ffn_kernel, tiles_per_seq=tiles_per_seq, final_norm=final_norm),
        grid=(t // rows,),
        in_specs=[pl.BlockSpec((rows, d), lambda i: (i, 0)),
                  pl.BlockSpec((HALO_ROWS, d), lambda i: (jnp.maximum(i * halo_per_tile - 1, 0), 0)),
                  pl.BlockSpec((HALO_ROWS, d), lambda i: (jnp.minimum((i + 1) * halo_per_tile, last_halo), 0)),
                  pl.BlockSpec((rows, d), lambda i: (i, 0)),
                  resident(wup_c), resident(cwb_c), resident(wdown), resident(gf)],
        out_specs=pl.BlockSpec((rows, d), lambda i: (i, 0)),
        out_shape=jax.ShapeDtypeStruct((t, d), F32),
        scratch_shapes=[pltpu.VMEM((nchunk, rows, cols2 // 2), BF16),
                        pltpu.VMEM((rows + 2 * HALO_ROWS, cols2), F32),
                        pltpu.VMEM((rows + 2 * HALO_ROWS, cols2), F32)],
        compiler_params=_cparams(("parallel",)),
        name="conv_ffn",
    )(h2, h2, h2, x1, wup_c, cwb_c, wdown, gf)


def _prep_ffn(w_up, conv_w, conv_b, cols):
    d, dff2 = w_up.shape
    dff = dff2 // 2
    nchunk = dff // cols

    def chunked(a):
        r = a.shape[0]
        g = a[:, :dff].reshape(r, nchunk, cols)
        v = a[:, dff:].reshape(r, nchunk, cols)
        return jnp.concatenate([g, v], axis=2).transpose(1, 0, 2)

    taps = jnp.concatenate([conv_w, conv_b[None, :], jnp.zeros((8 - CONV_W - 1, dff2), conv_w.dtype)], axis=0)
    return chunked(w_up).astype(BF16), chunked(taps)


def _rotate_half_cols(w):
    half = w.shape[1] // 2
    return jnp.concatenate([-w[:, half:], w[:, :half]], axis=1)


def _prep_layer(w_in, c_w_uq, c_w_ukv, w_out):
    d = w_in.shape[0]
    wa = w_in[:, 0:768]
    wb = w_in[:, 768:1536]
    c_q = w_in[:, 1536:1536 + C_Q_RANK]
    c_kv = w_in[:, 1792:1792 + C_KV_RANK]
    c_kr = w_in[:, 1920:1920 + C_ROPE]
    z = lambda n: jnp.zeros((d, n), w_in.dtype)
    wc = jnp.concatenate([c_q, c_kv, z(64), c_kr, z(32), z(64), _rotate_half_cols(c_kr), z(32)], axis=1)
    uq = c_w_uq.reshape(C_Q_RANK, C_HEADS, C_NOPE + C_ROPE)
    zq = lambda n: jnp.zeros((C_Q_RANK, C_HEADS, n), uq.dtype)
    w1 = jnp.concatenate([uq, zq(32)], axis=2).reshape(C_Q_RANK, C_HEADS * C_HEAD_PAD)
    uq_rope = uq[:, :, C_NOPE:]
    uq_rot = jnp.concatenate([-uq_rope[:, :, C_ROPE // 2:], uq_rope[:, :, :C_ROPE // 2]], axis=2)
    w2 = jnp.concatenate([zq(C_NOPE), uq_rot, zq(32)], axis=2).reshape(C_Q_RANK, C_HEADS * C_HEAD_PAD)
    ukv = c_w_ukv.reshape(C_KV_RANK, C_HEADS, C_NOPE + C_V)
    zk = jnp.zeros((C_KV_RANK, C_HEADS, C_HEAD_PAD - C_NOPE), ukv.dtype)
    wk = jnp.concatenate([ukv[:, :, :C_NOPE], zk], axis=2).reshape(C_KV_RANK, C_HEADS * C_HEAD_PAD)
    wv = ukv[:, :, C_NOPE:].reshape(C_KV_RANK, C_HEADS * C_V)
    bf = lambda a: a.astype(BF16)
    return dict(wa=bf(wa), wb=bf(wb), wc=bf(wc), w1=bf(w1), w2=bf(w2), wk=bf(wk), wv=bf(wv),
                woa=bf(w_out[0:256]), wob=bf(w_out[256:512]), woc=bf(w_out[512:1024]))


def _rope_tables(seq):
    pos = jnp.arange(seq, dtype=F32)
    inv_freq = ROPE_BASE ** (-jnp.arange(0, C_ROPE, 2, dtype=F32) / C_ROPE)
    ang = pos[:, None] * inv_freq[None, :]
    cos, sin = jnp.cos(ang), jnp.sin(ang)
    cos2 = jnp.concatenate([cos, cos], axis=1)
    sin2 = jnp.concatenate([sin, sin], axis=1)
    scale = (C_NOPE + C_ROPE) ** -0.5 * LOG2E
    ones = jnp.ones((seq, C_NOPE), F32)
    z = lambda n: jnp.zeros((seq, n), F32)
    cos_head = jnp.concatenate([ones, cos2, z(32)], axis=1)
    sin_head = jnp.concatenate([z(C_NOPE), sin2, z(32)], axis=1)
    cosq = jnp.tile(cos_head, (1, C_HEADS)) * scale
    sinq = jnp.tile(sin_head, (1, C_HEADS)) * scale
    cosk = jnp.concatenate([z(C_NOPE), cos2, z(32)], axis=1)
    sink = sin_head
    return cosq, sinq, cosk, sink


def kernel(x, w_in, g_attn, a_lq1, a_lk1, a_lq2, a_lk2, a_subln, c_g_q, c_w_uq, c_g_kv, c_w_ukv,
           w_out, g_ffn, w_up, conv_w, conv_b, w_down, g_final):
    batch, seq, d = x.shape
    depth = w_in.shape[0]
    assert seq % (16 * BAND_Q_ROWS) == 0 and seq % FFN_ROWS == 0 and d == 1024
    cosq, sinq, cosk, sink = _rope_tables(seq)
    x2 = x.reshape(batch * seq, d)
    row = lambda v: v[None, :]
    for l in range(depth):
        p = _prep_layer(w_in[l], c_w_uq[l], c_w_ukv[l], w_out[l])
        lam_init = 0.8 - 0.6 * math.exp(-0.3 * l)
        qkva, qkvb, qc, kc, vc = _proj_call(
            x2, row(g_attn[l]), p["wa"], p["wb"], p["wc"], row(c_g_q[l]), row(c_g_kv[l]),
            p["w1"], p["w2"], p["wk"], p["wv"], cosq, sinq, cosk, sink, seq)
        a_out = _diff_attn_call(qkva, row(a_lq1[l]), row(a_lk1[l]), row(a_lq2[l]), row(a_lk2[l]),
                                row(jnp.tile(a_subln[l], A_HEADS)), lam_init, batch, seq)
        b_out = _dilated_call(qkvb, batch, seq)
        c_out = _mla_call(qc, kc, vc, batch, seq)
        x1, h2 = _out_call(a_out, b_out, c_out, x2, p["woa"], p["wob"], p["woc"], row(g_ffn[l]))
        wup_c, cwb_c = _prep_ffn(w_up[l], conv_w[l], conv_b[l], FFN_COLS)
        x2 = _ffn_call(h2, x1, wup_c, cwb_c, w_down[l].astype(BF16), row(g_final), seq,
                       final_norm=(l == depth - 1))
    return x2.reshape(batch, seq, d)
```

```python
import functools
import math

import numpy as np
import jax
import jax.numpy as jnp
from jax import lax
from jax.experimental import pallas as pl
from jax.experimental.pallas import tpu as pltpu

A_HEADS, A_QK_DIM, A_V_DIM = 4, 32, 64
B_HEADS, B_DIM = 4, 64
B_DILATIONS = (1, 4, 16)
B_HALF = 64
N_BAND_OFFSETS = 3
C_HEADS, C_NOPE, C_ROPE, C_V = 8, 64, 32, 64
C_Q_RANK, C_KV_RANK = 256, 128
C_HEAD_PAD = 128
ROPE_BASE = 10000.0
CONV_W = 3
EPS = 1e-6
NEG = -1e30
LOG2E = math.log2(math.e)

_N_ALIBI = A_HEADS + B_HEADS
_SLOPES = [2.0 ** (-8.0 * i / _N_ALIBI) for i in range(1, _N_ALIBI + 1)]
SLOPES_A = tuple(_SLOPES[0::2])
SLOPES_B = tuple(_SLOPES[1::2])

PROJ_ROWS = 512
ATTN_Q_ROWS = 512
BAND_Q_ROWS = 128
FFN_ROWS = 512
FFN_COLS = 256
HALO_ROWS = 16
VMEM_LIMIT = 56 * 1024 * 1024

F32 = jnp.float32
BF16 = jnp.bfloat16


def _cparams(sem):
    return pltpu.CompilerParams(dimension_semantics=sem, vmem_limit_bytes=VMEM_LIMIT)


def _rms(x, g):
    return x * lax.rsqrt(jnp.mean(x * x, axis=-1, keepdims=True) + EPS) * g


def _dot(a, b):
    return jnp.dot(a, b, preferred_element_type=F32)


def _dot_nt(a, b):
    return lax.dot_general(a, b, (((1,), (1,)), ((), ())), preferred_element_type=F32)


def _lane_mask(width, lo, hi):
    lane = lax.broadcasted_iota(jnp.int32, (1, width), 1)
    return (lane >= lo) & (lane < hi)


def _proj_kernel(x_ref, g_ref, wa_ref, wb_ref, wc_ref, gq_ref, gkv_ref, w1_ref, w2_ref,
                 wk_ref, wv_ref, cosq_ref, sinq_ref, cosk_ref, sink_ref,
                 qkva_ref, qkvb_ref, qc_ref, kc_ref, vc_ref):
    h = _rms(x_ref[...], g_ref[...]).astype(BF16)
    pa = _dot(h, wa_ref[...])
    qw = A_HEADS * 2 * A_QK_DIM
    qkva_ref[:, 0:qw] = (pa[:, 0:qw] * (A_QK_DIM ** -0.5 * LOG2E)).astype(BF16)
    qkva_ref[:, qw:] = pa[:, qw:].astype(BF16)
    qkvb_ref[...] = _dot(h, wb_ref[...])
    pc = _dot(h, wc_ref[...])
    hq = _rms(pc[:, 0:C_Q_RANK], gq_ref[...]).astype(BF16)
    hkv = _rms(pc[:, C_Q_RANK:C_Q_RANK + C_KV_RANK], gkv_ref[...]).astype(BF16)
    kr = pc[:, 384:512]
    krs = pc[:, 512:640]
    q = _dot(hq, w1_ref[...]) * cosq_ref[...] + _dot(hq, w2_ref[...]) * sinq_ref[...]
    qc_ref[...] = q.astype(BF16)
    k_rope = kr * cosk_ref[...] + krs * sink_ref[...]
    kn = _dot(hkv, wk_ref[...])
    for hd in range(C_HEADS):
        sl = slice(hd * C_HEAD_PAD, (hd + 1) * C_HEAD_PAD)
        kc_ref[:, sl] = (kn[:, sl] + k_rope).astype(BF16)
    vc_ref[...] = _dot(hkv, wv_ref[...]).astype(BF16)


def _proj_call(x2, g, wa, wb, wc, gq, gkv, w1, w2, wk, wv, cosq, sinq, cosk, sink, seq):
    t, d = x2.shape
    rows = PROJ_ROWS
    nseq = seq // rows
    full = lambda a: pl.BlockSpec(a.shape, lambda i: (0, 0))
    tab = lambda a: pl.BlockSpec((rows, a.shape[1]), lambda i: (i % nseq, 0))
    row = lambda w: pl.BlockSpec((rows, w), lambda i: (i, 0))
    return pl.pallas_call(
        _proj_kernel,
        grid=(t // rows,),
        in_specs=[row(d), full(g), full(wa), full(wb), full(wc), full(gq), full(gkv),
                  full(w1), full(w2), full(wk), full(wv), tab(cosq), tab(sinq), tab(cosk), tab(sink)],
        out_specs=[row(768), row(768), row(1024), row(1024), row(512)],
        out_shape=[jax.ShapeDtypeStruct((t, 768), BF16), jax.ShapeDtypeStruct((t, 768), F32),
                   jax.ShapeDtypeStruct((t, 1024), BF16), jax.ShapeDtypeStruct((t, 1024), BF16),
                   jax.ShapeDtypeStruct((t, 512), BF16)],
        compiler_params=_cparams(("parallel",)),
        name="proj",
    )(x2, g, wa, wb, wc, gq, gkv, w1, w2, wk, wv, cosq, sinq, cosk, sink)


def _diff_attn_kernel(lq1_ref, lk1_ref, lq2_ref, lk2_ref, gsub_ref, q_ref, k_ref, v_ref, o_ref,
                      sa_scr, sb_scr, bias_scr, *, lam_init, seq):
    tq = q_ref.shape[0]
    width = q_ref.shape[1]
    i = pl.program_id(0)

    @pl.when(pl.program_id(1) == 0)
    def _():
        qpos = i * tq + lax.broadcasted_iota(jnp.int32, (tq, 1), 0)
        kpos = lax.broadcasted_iota(jnp.int32, (1, seq), 1)
        dist = jnp.abs(qpos - kpos).astype(F32)
        for hd in range(A_HEADS):
            bias_scr[hd] = dist * (-SLOPES_A[hd] * LOG2E)

    lam = (jnp.exp(jnp.sum(lq1_ref[...] * lk1_ref[...], axis=-1, keepdims=True))
           - jnp.exp(jnp.sum(lq2_ref[...] * lk2_ref[...], axis=-1, keepdims=True)) + lam_init)
    q = q_ref[...]
    k = k_ref[...]
    v = v_ref[...]
    gsub = gsub_ref[...]

    def scores(hd):
        base = hd * 2 * A_QK_DIM
        q12 = jnp.concatenate(
            [jnp.where(_lane_mask(width, lo, lo + A_QK_DIM), q, jnp.zeros_like(q))
             for lo in (base, base + A_QK_DIM)], axis=0)
        return _dot_nt(q12, k)

    bufs = (sa_scr, sb_scr)
    sa_scr[...] = scores(0)
    acc = jnp.zeros((tq, width), F32)
    for hd in range(A_HEADS):
        cur = bufs[hd % 2]
        if hd + 1 < A_HEADS:
            bufs[(hd + 1) % 2][...] = scores(hd + 1)

        def probs(rows):
            s = cur[rows, :] + bias_scr[hd]
            e = jnp.exp2(s - jnp.max(s, axis=-1, keepdims=True))
            return e.astype(BF16), jnp.sum(e, axis=-1, keepdims=True)

        e1, l1 = probs(slice(0, tq))
        e2, l2 = probs(slice(tq, 2 * tq))
        o12 = _dot(jnp.concatenate([e1, e2], axis=0), v)
        o = o12[0:tq, :] * (1.0 / l1) - o12[tq:2 * tq, :] * (lam / l2)
        mv = _lane_mask(width, hd * A_V_DIM, (hd + 1) * A_V_DIM)
        ms = jnp.sum(jnp.where(mv, o * o, 0.0), axis=-1, keepdims=True) * (1.0 / A_V_DIM)
        o = o * lax.rsqrt(ms + EPS) * gsub * (1.0 - lam_init)
        acc = jnp.where(mv, o, acc)
    o_ref[...] = acc.astype(BF16)


def _diff_attn_call(qkva, lq1, lk1, lq2, lk2, gsub, lam_init, batch, seq):
    tq = ATTN_Q_ROWS
    nq = seq // tq
    w = A_HEADS * A_V_DIM
    small = lambda a: pl.BlockSpec(a.shape, lambda i, b: (0, 0))
    return pl.pallas_call(
        functools.partial(_diff_attn_kernel, lam_init=lam_init, seq=seq),
        grid=(nq, batch),
        in_specs=[small(lq1), small(lk1), small(lq2), small(lk2), small(gsub),
                  pl.BlockSpec((tq, w), lambda i, b: (b * nq + i, 0)),
                  pl.BlockSpec((seq, w), lambda i, b: (b, 1)),
                  pl.BlockSpec((seq, w), lambda i, b: (b, 2))],
        out_specs=pl.BlockSpec((tq, w), lambda i, b: (b * nq + i, 0)),
        out_shape=jax.ShapeDtypeStruct((batch * seq, w), BF16),
        scratch_shapes=[pltpu.VMEM((2 * tq, seq), F32), pltpu.VMEM((2 * tq, seq), F32),
                        pltpu.VMEM((A_HEADS, tq, seq), F32)],
        compiler_params=_cparams(("parallel", "arbitrary")),
        name="diff_attn",
    )(lq1, lk1, lq2, lk2, gsub, qkva, qkva, qkva)


def _mla_kernel(q_ref, k_ref, v_ref, o_ref, sa_scr, sb_scr):
    tq = q_ref.shape[0]
    pair = 2 * C_V

    def scores(hd):
        sl = slice(hd * C_HEAD_PAD, (hd + 1) * C_HEAD_PAD)
        return _dot_nt(q_ref[:, sl], k_ref[:, sl])

    bufs = (sa_scr, sb_scr)
    sa_scr[...] = scores(0)
    for pr in range(C_HEADS // 2):
        vpair = v_ref[:, pr * pair:(pr + 1) * pair]
        acc = jnp.zeros((tq, pair), F32)
        for sub in range(2):
            hd = 2 * pr + sub
            if hd + 1 < C_HEADS:
                bufs[(hd + 1) % 2][...] = scores(hd + 1)
            s = bufs[hd % 2][...]
            e = jnp.exp2(s - jnp.max(s, axis=-1, keepdims=True))
            l = jnp.sum(e, axis=-1, keepdims=True)
            o = _dot(e.astype(BF16), vpair) * (1.0 / l)
            acc = jnp.where(_lane_mask(pair, sub * C_V, (sub + 1) * C_V), o, acc)
        o_ref[:, pr * pair:(pr + 1) * pair] = acc.astype(BF16)


def _mla_call(qc, kc, vc, batch, seq):
    tq = ATTN_Q_ROWS
    nq = seq // tq
    return pl.pallas_call(
        _mla_kernel,
        grid=(batch, nq),
        in_specs=[pl.BlockSpec((tq, qc.shape[1]), lambda b, i: (b * nq + i, 0)),
                  pl.BlockSpec((seq, kc.shape[1]), lambda b, i: (b, 0)),
                  pl.BlockSpec((seq, vc.shape[1]), lambda b, i: (b, 0))],
        out_specs=pl.BlockSpec((tq, vc.shape[1]), lambda b, i: (b * nq + i, 0)),
        out_shape=jax.ShapeDtypeStruct((batch * seq, vc.shape[1]), BF16),
        scratch_shapes=[pltpu.VMEM((tq, seq), F32), pltpu.VMEM((tq, seq), F32)],
        compiler_params=_cparams(("parallel", "parallel")),
        name="mla_attn",
    )(qc, kc, vc)


def _dilated_kernel(q0_ref, q1_ref, k0_ref, k1_ref, v0_ref, v1_ref, o_ref,
                    x4, qs, ks, vs, res_o, res_l, mrg_o, mrg_l, bias_tab, *, seq):
    in_refs = (q0_ref, q1_ref, k0_ref, k1_ref, v0_ref, v1_ref)
    operand = (qs, qs, ks, ks, vs, vs)
    half_w = q0_ref.shape[1]
    width = 2 * half_w
    tq = BAND_Q_ROWS
    nblk = seq // tq
    len4, len16 = seq // 4, seq // 16
    scale = B_DIM ** -0.5

    def put_operand(c6, dst_rows, val):
        lanes = slice((c6 % 2) * half_w, (c6 % 2 + 1) * half_w)
        if c6 < 2:
            val = val * scale
        operand[c6][dst_rows, lanes] = val.astype(BF16)

    def band_pass(dil):
        sub_len = seq // dil
        nqb = sub_len // tq
        win = min(2 * tq, sub_len)
        tab_base = B_DILATIONS.index(dil) * N_BAND_OFFSETS

        def body(t, carry):
            qb = t & (nqb - 1)
            kw = jnp.clip(qb * tq - B_HALF, 0, sub_len - win)
            krow = pl.multiple_of((t - qb) * tq + kw, B_HALF)
            qrow = pl.multiple_of(t * tq, tq)
            q = qs[pl.ds(qrow, tq), :]
            k = ks[pl.ds(krow, win), :]
            v = vs[pl.ds(krow, win), :]
            masks = [_lane_mask(width, hd * B_DIM, (hd + 1) * B_DIM) for hd in range(B_HEADS)]
            qstack = jnp.concatenate([jnp.where(mh, q, jnp.zeros_like(q)) for mh in masks], axis=0)
            offset_idx = lax.shift_right_logical(qb * tq - kw, B_HALF.bit_length() - 1)
            sc = _dot_nt(qstack, k) + bias_tab[tab_base + offset_idx, :, 0:win]
            m = jnp.max(sc, axis=-1, keepdims=True)
            p = jnp.exp(sc - m)
            den = jnp.sum(p, axis=-1, keepdims=True)
            o = _dot(p.astype(BF16), v) * (1.0 / den)
            lse = m + jnp.log(den)
            o_acc = jnp.zeros((tq, width), F32)
            l_acc = jnp.zeros((tq, width), F32)
            for hd in range(B_HEADS):
                rows_h = slice(hd * tq, (hd + 1) * tq)
                o_acc = jnp.where(masks[hd], o[rows_h, :], o_acc)
                l_acc = jnp.where(masks[hd], lse[rows_h, :], l_acc)
            for hf in range(2):
                lanes = slice(hf * half_w, (hf + 1) * half_w)
                res_o[hf, pl.ds(qrow, tq), :] = o_acc[:, lanes]
                res_l[hf, pl.ds(qrow, tq), :] = l_acc[:, lanes]
            return carry

        lax.fori_loop(0, nblk, body, 0, unroll=2)

    @pl.when(pl.program_id(0) == 0)
    def _():
        row = lax.broadcasted_iota(jnp.int32, (B_HEADS * tq, 1), 0)
        col = lax.broadcasted_iota(jnp.int32, (1, 2 * tq), 1)
        for di, dil in enumerate(B_DILATIONS):
            slope_col = jnp.zeros((B_HEADS * tq, 1), F32)
            for hd in range(B_HEADS):
                slope_col = jnp.where(row // tq == hd, -SLOPES_B[hd] * dil, slope_col)
            for oi in range(N_BAND_OFFSETS):
                rel = jnp.abs(oi * B_HALF + (row & (tq - 1)) - col)
                bias_tab[di * N_BAND_OFFSETS + oi, :, :] = jnp.where(
                    rel <= B_HALF, rel.astype(F32) * slope_col, NEG)

    def merge_into(dst_o, dst_l, write_out):
        chunk = 256

        def body(c, carry):
            rows = pl.ds(pl.multiple_of(c * chunk, chunk), chunk)
            for hf in range(2):
                la, lb = res_l[hf, rows, :], dst_l[hf, rows, :]
                mx = jnp.maximum(la, lb)
                wa, wb = jnp.exp(la - mx), jnp.exp(lb - mx)
                tot = wa + wb
                o = (wa * res_o[hf, rows, :] + wb * dst_o[hf, rows, :]) / tot
                if write_out:
                    o_ref[rows, hf * half_w:(hf + 1) * half_w] = o.astype(BF16)
                else:
                    dst_o[hf, rows, :] = o
                    dst_l[hf, rows, :] = mx + jnp.log(tot)
            return carry

        lax.fori_loop(0, seq // chunk, body, 0)

    for c6 in range(6):
        for r4 in range(4):
            x4[c6, r4 * len4:(r4 + 1) * len4, :] = in_refs[c6][pl.ds(r4, len4, stride=4), :]

    for c6 in range(6):
        for r16 in range(16):
            r4, c = r16 % 4, r16 // 4
            put_operand(c6, slice(r16 * len16, (r16 + 1) * len16),
                        x4[c6, pl.ds(r4 * len4 + c, len16, stride=4), :])
    band_pass(16)
    for r16 in range(16):
        r4, c = r16 % 4, r16 // 4
        for hf in range(2):
            src = slice(r16 * len16, (r16 + 1) * len16)
            mrg_o[hf, pl.ds(r4 * len4 + c, len16, stride=4), :] = res_o[hf, src, :]
            mrg_l[hf, pl.ds(r4 * len4 + c, len16, stride=4), :] = res_l[hf, src, :]

    for c6 in range(6):
        put_operand(c6, slice(0, seq), x4[c6, :, :])
    band_pass(4)
    merge_into(mrg_o, mrg_l, write_out=False)
    for r4 in range(4):
        for hf in range(2):
            src = slice(r4 * len4, (r4 + 1) * len4)
            x4[hf, pl.ds(r4, len4, stride=4), :] = mrg_o[hf, src, :]
            x4[2 + hf, pl.ds(r4, len4, stride=4), :] = mrg_l[hf, src, :]

    for c6 in range(6):
        put_operand(c6, slice(0, seq), in_refs[c6][...])
    band_pass(1)
    merge_into(x4.at[0:2], x4.at[2:4], write_out=True)


def _dilated_call(qkvb, batch, seq):
    w = B_HEADS * B_DIM
    hw = w // 2
    return pl.pallas_call(
        functools.partial(_dilated_kernel, seq=seq),
        grid=(batch,),
        in_specs=[pl.BlockSpec((seq, hw), functools.partial(lambda b, c: (b, c), c=c)) for c in range(6)],
        out_specs=pl.BlockSpec((seq, w), lambda b: (b, 0)),
        out_shape=jax.ShapeDtypeStruct((batch * seq, w), BF16),
        scratch_shapes=[pltpu.VMEM((6, seq, hw), F32),
                        pltpu.VMEM((seq, w), BF16), pltpu.VMEM((seq, w), BF16), pltpu.VMEM((seq, w), BF16),
                        pltpu.VMEM((2, seq, hw), F32), pltpu.VMEM((2, seq, hw), F32),
                        pltpu.VMEM((2, seq, hw), F32), pltpu.VMEM((2, seq, hw), F32),
                        pltpu.VMEM((len(B_DILATIONS) * N_BAND_OFFSETS, B_HEADS * BAND_Q_ROWS, 2 * BAND_Q_ROWS), F32)],
        compiler_params=_cparams(("arbitrary",)),
        name="dilated_attn",
    )(*([qkvb] * 6))


def _out_kernel(a_ref, b_ref, c_ref, x_ref, woa_ref, wob_ref, woc_ref, g_ref, x1_ref, h2_ref):
    x1 = (x_ref[...] + _dot(a_ref[...], woa_ref[...]) + _dot(b_ref[...], wob_ref[...])
          + _dot(c_ref[...], woc_ref[...]))
    x1_ref[...] = x1
    h2_ref[...] = _rms(x1, g_ref[...]).astype(BF16)


def _out_call(a, b, c, x2, woa, wob, woc, g):
    t, d = x2.shape
    rows = PROJ_ROWS
    full = lambda arr: pl.BlockSpec(arr.shape, lambda i: (0, 0))
    row = lambda w: pl.BlockSpec((rows, w), lambda i: (i, 0))
    return pl.pallas_call(
        _out_kernel,
        grid=(t // rows,),
        in_specs=[row(a.shape[1]), row(b.shape[1]), row(c.shape[1]), row(d),
                  full(woa), full(wob), full(woc), full(g)],
        out_specs=[row(d), row(d)],
        out_shape=[jax.ShapeDtypeStruct((t, d), F32), jax.ShapeDtypeStruct((t, d), BF16)],
        compiler_params=_cparams(("parallel",)),
        name="out_proj",
    )(a, b, c, x2, woa, wob, woc, g)


def _ffn_kernel(h_ref, hp_ref, hn_ref, x1_ref, wup_ref, cwb_ref, wd_ref, gf_ref, o_ref,
                act_scr, ua_scr, ub_scr, *, tiles_per_seq, final_norm):
    rows = h_ref.shape[0]
    nchunk, _, cols2 = wup_ref.shape
    cols = cols2 // 2
    si = pl.program_id(0) % tiles_per_seq
    has_prev = si > 0
    has_next = si < tiles_per_seq - 1
    h = jnp.concatenate([h_ref[...], hp_ref[...], hn_ref[...]], axis=0)
    ridx = lax.broadcasted_iota(jnp.int32, (rows, 1), 0)

    def conv_gate(j, u_all):
        u = u_all[0:rows, :]
        u_before = u_all[rows + HALO_ROWS - 1:rows + HALO_ROWS, :]
        u_after = u_all[rows + HALO_ROWS:rows + HALO_ROWS + 1, :]
        u_before = jnp.where(has_prev, u_before, 0.0)
        u_after = jnp.where(has_next, u_after, 0.0)
        u_dn = jnp.where(ridx == 0, u_before, pltpu.roll(u, 1, axis=0))
        u_up = jnp.where(ridx == rows - 1, u_after, pltpu.roll(u, rows - 1, axis=0))
        cwb = cwb_ref[j]
        c = u_dn * cwb[0:1, :] + u * cwb[1:2, :] + u_up * cwb[2:3, :] + cwb[3:4, :]
        gate, val = c[:, 0:cols], c[:, cols:cols2]
        act_scr[j] = (gate * jax.nn.sigmoid(gate) * val).astype(BF16)

    assert nchunk % 2 == 1
    ua_scr[...] = _dot(h, wup_ref[0])

    def chunk_pair(jj, carry):
        j = 2 * jj
        ua = ua_scr[...]
        ub_scr[...] = _dot(h, wup_ref[j + 1])
        conv_gate(j, ua)
        ub = ub_scr[...]
        ua_scr[...] = _dot(h, wup_ref[j + 2])
        conv_gate(j + 1, ub)
        return carry

    lax.fori_loop(0, nchunk // 2, chunk_pair, 0)
    conv_gate(nchunk - 1, ua_scr[...])
    act =jnp.concatenate([act_scr[j] for j in range(nchunk)], axis=1)
    out = x1_ref[...] + _dot(act, wd_ref[...])
    if final_norm:
        out = _rms(out, gf_ref[...])
    o_ref[...] = out


def _ffn_call(h2, x1, wup_c, cwb_c, wdown, gf, seq, final_norm):
    t, d = x1.shape
    rows = FFN_ROWS
    nchunk, _, cols2 = wup_c.shape
    tiles_per_seq = seq // rows
    halo_per_tile = rows // HALO_ROWS
    last_halo = t // HALO_ROWS - 1
    resident = lambda a: pl.BlockSpec(a.shape, lambda i: (0,) * a.ndim, pipeline_mode=pl.Buffered(1))
    return pl.pallas_call(
        functools.partial(_ffn_kernel, tiles_per_seq=tiles_per_seq, final_norm=final_norm),
        grid=(t // rows,),
        in_specs=[pl.BlockSpec((rows, d), lambda i: (i, 0)),
                  pl.BlockSpec((HALO_ROWS, d), lambda i: (jnp.maximum(i * halo_per_tile - 1, 0), 0)),
                  pl.BlockSpec((HALO_ROWS, d), lambda i: (jnp.minimum((i + 1) * halo_per_tile, last_halo), 0)),
                  pl.BlockSpec((rows, d), lambda i: (i, 0)),
                  resident(wup_c), resident(cwb_c), resident(wdown), resident(gf)],
        out_specs=pl.BlockSpec((rows, d), lambda i: (i, 0)),
        out_shape=jax.ShapeDtypeStruct((t, d), F32),
        scratch_shapes=[pltpu.VMEM((nchunk, rows, cols2 // 2), BF16),
                        pltpu.VMEM((rows + 2 * HALO_ROWS, cols2), F32),
                        pltpu.VMEM((rows + 2 * HALO_ROWS, cols2), F32)],
        compiler_params=_cparams(("parallel",)),
        name="conv_ffn",
    )(h2, h2, h2, x1, wup_c, cwb_c, wdown, gf)


def _prep_ffn(w_up, conv_w, conv_b, cols):
    d, dff2 = w_up.shape
    dff = dff2 // 2
    nchunk = dff // cols

    def chunked(a):
        r = a.shape[0]
        g = a[:, :dff].reshape(r, nchunk, cols)
        v = a[:, dff:].reshape(r, nchunk, cols)
        return jnp.concatenate([g, v], axis=2).transpose(1, 0, 2)

    taps = jnp.concatenate([conv_w, conv_b[None, :], jnp.zeros((8 - CONV_W - 1, dff2), conv_w.dtype)], axis=0)
    return chunked(w_up).astype(BF16), chunked(taps)


def _rotate_half_cols(w):
    half = w.shape[1] // 2
    return jnp.concatenate([-w[:, half:], w[:, :half]], axis=1)


def _prep_layer(w_in, c_w_uq, c_w_ukv, w_out):
    d = w_in.shape[0]
    wa = w_in[:, 0:768]
    wb = w_in[:, 768:1536]
    c_q = w_in[:, 1536:1536 + C_Q_RANK]
    c_kv = w_in[:, 1792:1792 + C_KV_RANK]
    c_kr = w_in[:, 1920:1920 + C_ROPE]
    z = lambda n: jnp.zeros((d, n), w_in.dtype)
    wc = jnp.concatenate([c_q, c_kv, z(64), c_kr, z(32), z(64), _rotate_half_cols(c_kr), z(32)], axis=1)
    uq = c_w_uq.reshape(C_Q_RANK, C_HEADS, C_NOPE + C_ROPE)
    zq = lambda n: jnp.zeros((C_Q_RANK, C_HEADS, n), uq.dtype)
    w1 = jnp.concatenate([uq, zq(32)], axis=2).reshape(C_Q_RANK, C_HEADS * C_HEAD_PAD)
    uq_rope = uq[:, :, C_NOPE:]
    uq_rot = jnp.concatenate([-uq_rope[:, :, C_ROPE // 2:], uq_rope[:, :, :C_ROPE // 2]], axis=2)
    w2 = jnp.concatenate([zq(C_NOPE), uq_rot, zq(32)], axis=2).reshape(C_Q_RANK, C_HEADS * C_HEAD_PAD)
    ukv = c_w_ukv.reshape(C_KV_RANK, C_HEADS, C_NOPE + C_V)
    zk = jnp.zeros((C_KV_RANK, C_HEADS, C_HEAD_PAD - C_NOPE), ukv.dtype)
    wk = jnp.concatenate([ukv[:, :, :C_NOPE], zk], axis=2).reshape(C_KV_RANK, C_HEADS * C_HEAD_PAD)
    wv = ukv[:, :, C_NOPE:].reshape(C_KV_RANK, C_HEADS * C_V)
    bf = lambda a: a.astype(BF16)
    return dict(wa=bf(wa), wb=bf(wb), wc=bf(wc), w1=bf(w1), w2=bf(w2), wk=bf(wk), wv=bf(wv),
                woa=bf(w_out[0:256]), wob=bf(w_out[256:512]), woc=bf(w_out[512:1024]))


def _rope_tables(seq):
    pos = jnp.arange(seq, dtype=F32)
    inv_freq = ROPE_BASE ** (-jnp.arange(0, C_ROPE, 2, dtype=F32) / C_ROPE)
    ang = pos[:, None] * inv_freq[None, :]
    cos, sin = jnp.cos(ang), jnp.sin(ang)
    cos2 = jnp.concatenate([cos, cos], axis=1)
    sin2 = jnp.concatenate([sin, sin], axis=1)
    scale = (C_NOPE + C_ROPE) ** -0.5 * LOG2E
    ones = jnp.ones((seq, C_NOPE), F32)
    z = lambda n: jnp.zeros((seq, n), F32)
    cos_head = jnp.concatenate([ones, cos2, z(32)], axis=1)
    sin_head = jnp.concatenate([z(C_NOPE), sin2, z(32)], axis=1)
    cosq = jnp.tile(cos_head, (1, C_HEADS)) * scale
    sinq = jnp.tile(sin_head, (1, C_HEADS)) * scale
    cosk = jnp.concatenate([z(C_NOPE), cos2, z(32)], axis=1)
    sink = sin_head
    return cosq, sinq, cosk, sink


def kernel(x, w_in, g_attn, a_lq1, a_lk1, a_lq2, a_lk2, a_subln, c_g_q, c_w_uq, c_g_kv, c_w_ukv,
           w_out, g_ffn, w_up, conv_w, conv_b, w_down, g_final):
    batch, seq, d = x.shape
    depth = w_in.shape[0]
    assert seq % (16 * BAND_Q_ROWS) == 0 and seq % FFN_ROWS == 0 and d == 1024
    cosq, sinq, cosk, sink = _rope_tables(seq)
    x2 = x.reshape(batch * seq, d)
    row = lambda v: v[None, :]
    for l in range(depth):
        p = _prep_layer(w_in[l], c_w_uq[l], c_w_ukv[l], w_out[l])
        lam_init = 0.8 - 0.6 * math.exp(-0.3 * l)
        qkva, qkvb, qc, kc, vc = _proj_call(
            x2, row(g_attn[l]), p["wa"], p["wb"], p["wc"], row(c_g_q[l]), row(c_g_kv[l]),
            p["w1"], p["w2"], p["wk"], p["wv"], cosq, sinq, cosk, sink, seq)
        a_out = _diff_attn_call(qkva, row(a_lq1[l]), row(a_lk1[l]), row(a_lq2[l]), row(a_lk2[l]),
                                row(jnp.tile(a_subln[l], A_HEADS)), lam_init, batch, seq)
        b_out = _dilated_call(qkvb, batch, seq)
        c_out = _mla_call(qc, kc, vc, batch, seq)
        x1, h2 = _out_call(a_out, b_out, c_out, x2, p["woa"], p["wob"], p["woc"], row(g_ffn[l]))
        wup_c, cwb_c = _prep_ffn(w_up[l], conv_w[l], conv_b[l], FFN_COLS)
        x2 = _ffn_call(h2, x1, wup_c, cwb_c, w_down[l].astype(BF16), row(g_final), seq,
                       final_norm=(l == depth - 1))
    return x2.reshape(batch, seq, d)
```

```python
import functools
import math

import numpy as np
import jax
import jax.numpy as jnp
from jax import lax
from jax.experimental import pallas as pl
from jax.experimental.pallas import tpu as pltpu

A_HEADS, A_QK_DIM, A_V_DIM = 4, 32, 64
B_HEADS, B_DIM = 4, 64
B_DILATIONS = (1, 4, 16)
B_HALF = 64
N_BAND_OFFSETS = 3
C_HEADS, C_NOPE, C_ROPE, C_V = 8, 64, 32, 64
C_Q_RANK, C_KV_RANK = 256, 128
C_HEAD_PAD = 128
ROPE_BASE = 10000.0
CONV_W = 3
EPS = 1e-6
NEG = -1e30
LOG2E = math.log2(math.e)

_N_ALIBI = A_HEADS + B_HEADS
_SLOPES = [2.0 ** (-8.0 * i / _N_ALIBI) for i in range(1, _N_ALIBI + 1)]
SLOPES_A = tuple(_SLOPES[0::2])
SLOPES_B = tuple(_SLOPES[1::2])

PROJ_ROWS = 512
ATTN_Q_ROWS = 512
BAND_Q_ROWS = 128
FFN_ROWS = 512
FFN_COLS = 256
LANES = 128
HALO_ROWS = 16
VMEM_LIMIT = 56 * 1024 * 1024

F32 = jnp.float32
BF16 = jnp.bfloat16


def _cparams(sem):
    return pltpu.CompilerParams(dimension_semantics=sem, vmem_limit_bytes=VMEM_LIMIT)


def _rms(x, g):
    return x * lax.rsqrt(jnp.mean(x * x, axis=-1, keepdims=True) + EPS) * g


def _dot(a, b):
    return jnp.dot(a, b, preferred_element_type=F32)


def _dot_nt(a, b):
    return lax.dot_general(a, b, (((1,), (1,)), ((), ())), preferred_element_type=F32)


def _lane_mask(width, lo, hi):
    lane = lax.broadcasted_iota(jnp.int32, (1, width), 1)
    return (lane >= lo) & (lane < hi)


def _proj_kernel(x_ref, g_ref, wa_ref, wb_ref, wc_ref, gq_ref, gkv_ref, w1_ref, w2_ref,
                 wk_ref, wv_ref, cosq_ref, sinq_ref, cosk_ref, sink_ref,
                 qkva_ref, qkvb_ref, qc_ref, kc_ref, vc_ref):
    h = _rms(x_ref[...], g_ref[...]).astype(BF16)
    pa = _dot(h, wa_ref[...])
    qw = A_HEADS * 2 * A_QK_DIM
    qkva_ref[:, 0:qw] = (pa[:, 0:qw] * (A_QK_DIM ** -0.5 * LOG2E)).astype(BF16)
    qkva_ref[:, qw:] = pa[:, qw:].astype(BF16)
    qkvb_ref[...] = _dot(h, wb_ref[...])
    pc = _dot(h, wc_ref[...])
    hq = _rms(pc[:, 0:C_Q_RANK], gq_ref[...]).astype(BF16)
    hkv = _rms(pc[:, C_Q_RANK:C_Q_RANK + C_KV_RANK], gkv_ref[...]).astype(BF16)
    kr = pc[:, 384:512]
    krs = pc[:, 512:640]
    q = _dot(hq, w1_ref[...]) * cosq_ref[...] + _dot(hq, w2_ref[...]) * sinq_ref[...]
    qc_ref[...] = q.astype(BF16)
    k_rope = kr * cosk_ref[...] + krs * sink_ref[...]
    kn = _dot(hkv, wk_ref[...])
    for hd in range(C_HEADS):
        sl = slice(hd * C_HEAD_PAD, (hd + 1) * C_HEAD_PAD)
        kc_ref[:, sl] = (kn[:, sl] + k_rope).astype(BF16)
    vc = _dot(hkv, wv_ref[...])
    lane = lax.broadcasted_iota(jnp.int32, (1, vc.shape[1]), 1)
    vc_ref[...] = jnp.where((lane & (4 * C_V - 1)) >= 2 * C_V, 1.0, vc).astype(BF16)


def _proj_call(x2, g, wa, wb, wc, gq, gkv, w1, w2, wk, wv, cosq, sinq, cosk, sink, seq):
    t, d = x2.shape
    rows = PROJ_ROWS
    nseq = seq // rows
    full = lambda a: pl.BlockSpec(a.shape, lambda i: (0, 0))
    tab = lambda a: pl.BlockSpec((rows, a.shape[1]), lambda i: (i % nseq, 0))
    row = lambda w: pl.BlockSpec((rows, w), lambda i: (i, 0))
    return pl.pallas_call(
        _proj_kernel,
        grid=(t // rows,),
        in_specs=[row(d), full(g), full(wa), full(wb), full(wc), full(gq), full(gkv),
                  full(w1), full(w2), full(wk), full(wv), tab(cosq), tab(sinq), tab(cosk), tab(sink)],
        out_specs=[row(768), row(768), row(1024), row(1024), row(1024)],
        out_shape=[jax.ShapeDtypeStruct((t, 768), BF16), jax.ShapeDtypeStruct((t, 768), F32),
                   jax.ShapeDtypeStruct((t, 1024), BF16), jax.ShapeDtypeStruct((t, 1024), BF16),
                   jax.ShapeDtypeStruct((t, 1024), BF16)],
        compiler_params=_cparams(("parallel",)),
        name="proj",
    )(x2, g, wa, wb, wc, gq, gkv, w1, w2, wk, wv, cosq, sinq, cosk, sink)


def _diff_attn_kernel(lq1_ref, lk1_ref, lq2_ref, lk2_ref, gsub_ref, q_ref, k_ref, v_ref, o_ref,
                      sa_scr, sb_scr, bias_scr, *, lam_init, seq):
    tq = q_ref.shape[0]
    width = q_ref.shape[1]
    i = pl.program_id(0)

    @pl.when(pl.program_id(1) == 0)
    def _():
        qpos = i * tq + lax.broadcasted_iota(jnp.int32, (tq, 1), 0)
        kpos = lax.broadcasted_iota(jnp.int32, (1, seq), 1)
        dist = jnp.abs(qpos - kpos).astype(F32)
        for hd in range(A_HEADS):
            bias_scr[hd] = dist * (-SLOPES_A[hd] * LOG2E)

    lam = (jnp.exp(jnp.sum(lq1_ref[...] * lk1_ref[...], axis=-1, keepdims=True))
           - jnp.exp(jnp.sum(lq2_ref[...] * lk2_ref[...], axis=-1, keepdims=True)) + lam_init)
    q = q_ref[...]
    k = k_ref[...]
    v = v_ref[...]
    gsub = gsub_ref[...]

    def scores(hd):
        base = hd * 2 * A_QK_DIM
        q12 = jnp.concatenate(
            [jnp.where(_lane_mask(width, lo, lo + A_QK_DIM), q, jnp.zeros_like(q))
             for lo in (base, base + A_QK_DIM)], axis=0)
        return _dot_nt(q12, k)

    bufs = (sa_scr, sb_scr)
    sa_scr[...] = scores(0)
    acc = [jnp.zeros((tq, LANES), F32) for _ in range(width // LANES)]
    for hd in range(A_HEADS):
        cur = bufs[hd % 2]
        if hd + 1 < A_HEADS:
            bufs[(hd + 1) % 2][...] = scores(hd + 1)

        def probs(rows):
            s = cur[rows, :] + bias_scr[hd]
            return jnp.exp2(s - jnp.max(s, axis=-1, keepdims=True)).astype(BF16)

        mv = _lane_mask(width, hd * A_V_DIM, (hd + 1) * A_V_DIM)
        v_h = jnp.where(mv, v, jnp.ones_like(v))
        o12 = _dot(jnp.concatenate([probs(slice(0, tq)), probs(slice(tq, 2 * tq))], axis=0), v_h)
        tile = (hd * A_V_DIM) // LANES
        val = o12[:, tile * LANES:(tile + 1) * LANES]
        den = o12[:, (1 - tile) * LANES:(2 - tile) * LANES]
        o = val[0:tq, :] * (1.0 / den[0:tq, :]) - val[tq:2 * tq, :] * (lam / den[tq:2 * tq, :])
        mh = _lane_mask(LANES, (hd * A_V_DIM) % LANES, (hd * A_V_DIM) % LANES + A_V_DIM)
        ms = jnp.sum(jnp.where(mh, o * o, 0.0), axis=-1, keepdims=True) * (1.0 / A_V_DIM)
        o = o * lax.rsqrt(ms + EPS) * gsub[:, tile * LANES:(tile + 1) * LANES] * (1.0 - lam_init)
        acc[tile] = jnp.where(mh, o, acc[tile])
    for tile in range(width // LANES):
        o_ref[:, tile * LANES:(tile + 1) * LANES] = acc[tile].astype(BF16)


def _diff_attn_call(qkva, lq1, lk1, lq2, lk2, gsub, lam_init, batch, seq):
    tq = ATTN_Q_ROWS
    nq = seq // tq
    w = A_HEADS * A_V_DIM
    small = lambda a: pl.BlockSpec(a.shape, lambda i, b: (0, 0))
    return pl.pallas_call(
        functools.partial(_diff_attn_kernel, lam_init=lam_init, seq=seq),
        grid=(nq, batch),
        in_specs=[small(lq1), small(lk1), small(lq2), small(lk2), small(gsub),
                  pl.BlockSpec((tq, w), lambda i, b: (b * nq + i, 0)),
                  pl.BlockSpec((seq, w), lambda i, b: (b, 1)),
                  pl.BlockSpec((seq, w), lambda i, b: (b, 2))],
        out_specs=pl.BlockSpec((tq, w), lambda i, b: (b * nq + i, 0)),
        out_shape=jax.ShapeDtypeStruct((batch * seq, w), BF16),
        scratch_shapes=[pltpu.VMEM((2 * tq, seq), F32), pltpu.VMEM((2 * tq, seq), F32),
                        pltpu.VMEM((A_HEADS, tq, seq), F32)],
        compiler_params=_cparams(("parallel", "arbitrary")),
        name="diff_attn",
    )(lq1, lk1, lq2, lk2, gsub, qkva, qkva, qkva)


def _mla_kernel(q_ref, k_ref, v_ref, o_ref, sa_scr, sb_scr):
    tq = q_ref.shape[0]
    pair = 2 * C_V

    def scores(hd):
        sl = slice(hd * C_HEAD_PAD, (hd + 1) * C_HEAD_PAD)
        return _dot_nt(q_ref[:, sl], k_ref[:, sl])

    bufs = (sa_scr, sb_scr)
    sa_scr[...] = scores(0)
    for pr in range(C_HEADS // 2):
        vpair = v_ref[:, pr * 2 * pair:(pr + 1) * 2 * pair]
        acc = jnp.zeros((tq, pair), F32)
        for sub in range(2):
            hd = 2 * pr + sub
            if hd + 1 < C_HEADS:
                bufs[(hd + 1) % 2][...] = scores(hd + 1)
            s = bufs[hd % 2][...]
            e = jnp.exp2(s - jnp.max(s, axis=-1, keepdims=True))
            ol = _dot(e.astype(BF16), vpair)
            o = ol[:, 0:pair] * (1.0 / ol[:, pair:2 * pair])
            acc = jnp.where(_lane_mask(pair, sub * C_V, (sub + 1) * C_V), o, acc)
        o_ref[:, pr * pair:(pr + 1) * pair] = acc.astype(BF16)


def _mla_call(qc, kc, vc, batch, seq):
    tq = ATTN_Q_ROWS
    nq = seq // tq
    return pl.pallas_call(
        _mla_kernel,
        grid=(batch, nq),
        in_specs=[pl.BlockSpec((tq, qc.shape[1]), lambda b, i: (b * nq + i, 0)),
                  pl.BlockSpec((seq, kc.shape[1]), lambda b, i: (b, 0)),
                  pl.BlockSpec((seq, vc.shape[1]), lambda b, i: (b, 0))],
        out_specs=pl.BlockSpec((tq, C_HEADS * C_V), lambda b, i: (b * nq + i, 0)),
        out_shape=jax.ShapeDtypeStruct((batch * seq, C_HEADS * C_V), BF16),
        scratch_shapes=[pltpu.VMEM((tq, seq), F32), pltpu.VMEM((tq, seq), F32)],
        compiler_params=_cparams(("parallel", "parallel")),
        name="mla_attn",
    )(qc, kc, vc)


def _dilated_kernel(q0_ref, q1_ref, k0_ref, k1_ref, v0_ref, v1_ref, o_ref,
                    x4, qs, ks, vs, res_o, res_l, mrg_o, mrg_l, bias_tab, *, seq):
    in_refs = (q0_ref, q1_ref, k0_ref, k1_ref, v0_ref, v1_ref)
    operand = (qs, qs, ks, ks, vs, vs)
    half_w = q0_ref.shape[1]
    width = 2 * half_w
    tq = BAND_Q_ROWS
    nblk = seq // tq
    len4, len16 = seq // 4, seq // 16
    scale = B_DIM ** -0.5

    def put_operand(c6, dst_rows, val):
        lanes = slice((c6 % 2) * half_w, (c6 % 2 + 1) * half_w)
        if c6 < 2:
            val = val * scale
        operand[c6][dst_rows, lanes] = val.astype(BF16)

    def band_pass(dil):
        sub_len = seq // dil
        nqb = sub_len // tq
        win = min(2 * tq, sub_len)
        tab_base = B_DILATIONS.index(dil) * N_BAND_OFFSETS

        def body(t, carry):
            qb = t & (nqb - 1)
            kw = jnp.clip(qb * tq - B_HALF, 0, sub_len - win)
            krow = pl.multiple_of((t - qb) * tq + kw, B_HALF)
            qrow = pl.multiple_of(t * tq, tq)
            q = qs[pl.ds(qrow, tq), :]
            k = ks[pl.ds(krow, win), :]
            v = vs[pl.ds(krow, win), :]
            masks = [_lane_mask(width, hd * B_DIM, (hd + 1) * B_DIM) for hd in range(B_HEADS)]
            qstack = jnp.concatenate([jnp.where(mh, q, jnp.zeros_like(q)) for mh in masks], axis=0)
            offset_idx = lax.shift_right_logical(qb * tq - kw, B_HALF.bit_length() - 1)
            sc = _dot_nt(qstack, k) + bias_tab[tab_base + offset_idx, :, 0:win]
            m = jnp.max(sc, axis=-1, keepdims=True)
            p = jnp.exp(sc - m)
            den = jnp.sum(p, axis=-1, keepdims=True)
            o = _dot(p.astype(BF16), v) * (1.0 / den)
            lse = m + jnp.log(den)
            o_acc = jnp.zeros((tq, width), F32)
            l_acc = jnp.zeros((tq, width), F32)
            for hd in range(B_HEADS):
                rows_h = slice(hd * tq, (hd + 1) * tq)
                o_acc = jnp.where(masks[hd], o[rows_h, :], o_acc)
                l_acc = jnp.where(masks[hd], lse[rows_h, :], l_acc)
            for hf in range(2):
                lanes = slice(hf * half_w, (hf + 1) * half_w)
                res_o[hf, pl.ds(qrow, tq), :] = o_acc[:, lanes]
                res_l[hf, pl.ds(qrow, tq), :] = l_acc[:, lanes]
            return carry

        lax.fori_loop(0, nblk, body, 0, unroll=2)

    @pl.when(pl.program_id(0) == 0)
    def _():
        row = lax.broadcasted_iota(jnp.int32, (B_HEADS * tq, 1), 0)
        col = lax.broadcasted_iota(jnp.int32, (1, 2 * tq), 1)
        for di, dil in enumerate(B_DILATIONS):
            slope_col = jnp.zeros((B_HEADS * tq, 1), F32)
            for hd in range(B_HEADS):
                slope_col = jnp.where(row // tq == hd, -SLOPES_B[hd] * dil, slope_col)
            for oi in range(N_BAND_OFFSETS):
                rel = jnp.abs(oi * B_HALF + (row & (tq - 1)) - col)
                bias_tab[di * N_BAND_OFFSETS + oi, :, :] = jnp.where(
                    rel <= B_HALF, rel.astype(F32) * slope_col, NEG)

    def merge_into(dst_o, dst_l, write_out):
        chunk = 256

        def body(c, carry):
            rows = pl.ds(pl.multiple_of(c * chunk, chunk), chunk)
            for hf in range(2):
                la, lb = res_l[hf, rows, :], dst_l[hf, rows, :]
                mx = jnp.maximum(la, lb)
                wa, wb = jnp.exp(la - mx), jnp.exp(lb - mx)
                tot = wa + wb
                o = (wa * res_o[hf, rows, :] + wb * dst_o[hf, rows, :]) / tot
                if write_out:
                    o_ref[rows, hf * half_w:(hf + 1) * half_w] = o.astype(BF16)
                else:
                    dst_o[hf, rows, :] = o
                    dst_l[hf, rows, :] = mx + jnp.log(tot)
            return carry

        lax.fori_loop(0, seq // chunk, body, 0)

    for c6 in range(6):
        for r4 in range(4):
            x4[c6, r4 * len4:(r4 + 1) * len4, :] = in_refs[c6][pl.ds(r4, len4, stride=4), :]

    for c6 in range(6):
        for r16 in range(16):
            r4, c = r16 % 4, r16 // 4
            put_operand(c6, slice(r16 * len16, (r16 + 1) * len16),
                        x4[c6, pl.ds(r4 * len4 + c, len16, stride=4), :])
    band_pass(16)
    for r16 in range(16):
        r4, c = r16 % 4, r16 // 4
        for hf in range(2):
            src = slice(r16 * len16, (r16 + 1) * len16)
            mrg_o[hf, pl.ds(r4 * len4 + c, len16, stride=4), :] = res_o[hf, src, :]
            mrg_l[hf, pl.ds(r4 * len4 + c, len16, stride=4), :] = res_l[hf, src, :]

    for c6 in range(6):
        put_operand(c6, slice(0, seq), x4[c6, :, :])
    band_pass(4)
    merge_into(mrg_o, mrg_l, write_out=False)
    for r4 in range(4):
        for hf in range(2):
            src = slice(r4 * len4, (r4 + 1) * len4)
            x4[hf, pl.ds(r4, len4, stride=4), :] = mrg_o[hf, src, :]
            x4[2 + hf, pl.ds(r4, len4, stride=4), :] = mrg_l[hf, src, :]

    for c6 in range(6):
        put_operand(c6, slice(0, seq), in_refs[c6][...])
    band_pass(1)
    merge_into(x4.at[0:2], x4.at[2:4], write_out=True)


def _dilated_call(qkvb, batch, seq):
    w = B_HEADS * B_DIM
    hw = w // 2
    return pl.pallas_call(
        functools.partial(_dilated_kernel, seq=seq),
        grid=(batch,),
        in_specs=[pl.BlockSpec((seq, hw), functools.partial(lambda b, c: (b, c), c=c)) for c in range(6)],
        out_specs=pl.BlockSpec((seq, w), lambda b: (b, 0)),
        out_shape=jax.ShapeDtypeStruct((batch * seq, w), BF16),
        scratch_shapes=[pltpu.VMEM((6, seq, hw), F32),
                        pltpu.VMEM((seq, w), BF16), pltpu.VMEM((seq, w), BF16), pltpu.VMEM((seq, w), BF16),
                        pltpu.VMEM((2, seq, hw), F32), pltpu.VMEM((2, seq, hw), F32),
                        pltpu.VMEM((2, seq, hw), F32), pltpu.VMEM((2, seq, hw), F32),
                        pltpu.VMEM((len(B_DILATIONS) * N_BAND_OFFSETS, B_HEADS * BAND_Q_ROWS, 2 * BAND_Q_ROWS), F32)],
        compiler_params=_cparams(("arbitrary",)),
        name="dilated_attn",
    )(*([qkvb] * 6))


def _out_kernel(a_ref, b_ref, c_ref, x_ref, woa_ref, wob_ref, woc_ref, g_ref, x1_ref, h2_ref):
    x1 = (x_ref[...] + _dot(a_ref[...], woa_ref[...]) + _dot(b_ref[...], wob_ref[...])
          + _dot(c_ref[...], woc_ref[...]))
    x1_ref[...] = x1
    h2_ref[...] = _rms(x1, g_ref[...]).astype(BF16)


def _out_call(a, b, c, x2, woa, wob, woc, g):
    t, d = x2.shape
    rows = PROJ_ROWS
    full = lambda arr: pl.BlockSpec(arr.shape, lambda i: (0, 0))
    row = lambda w: pl.BlockSpec((rows, w), lambda i: (i, 0))
    return pl.pallas_call(
        _out_kernel,
        grid=(t // rows,),
        in_specs=[row(a.shape[1]), row(b.shape[1]), row(c.shape[1]), row(d),
                  full(woa), full(wob), full(woc), full(g)],
        out_specs=[row(d), row(d)],
        out_shape=[jax.ShapeDtypeStruct((t, d), F32), jax.ShapeDtypeStruct((t, d), BF16)],
        compiler_params=_cparams(("parallel",)),
        name="out_proj",
    )(a, b, c, x2, woa, wob, woc, g)


def _ffn_kernel(h_ref, hp_ref, hn_ref, x1_ref, wup_ref, cwb_ref, wd_ref, gf_ref, o_ref,
                act_scr, ua_scr, ub_scr, *, tiles_per_seq, final_norm):
    rows = h_ref.shape[0]
    nchunk, _, cols2 = wup_ref.shape
    cols = cols2 // 2
    si = pl.program_id(0) % tiles_per_seq
    has_prev = si > 0
    has_next = si < tiles_per_seq - 1
    h = jnp.concatenate([h_ref[...], hp_ref[...], hn_ref[...]], axis=0)
    ridx = lax.broadcasted_iota(jnp.int32, (rows, 1), 0)

    def conv_gate(j, u_all):
        u = u_all[0:rows, :]
        u_before = u_all[rows + HALO_ROWS - 1:rows + HALO_ROWS, :]
        u_after = u_all[rows + HALO_ROWS:rows + HALO_ROWS + 1, :]
        u_before = jnp.where(has_prev, u_before, 0.0)
        u_after = jnp.where(has_next, u_after, 0.0)
        u_dn = jnp.where(ridx == 0, u_before, pltpu.roll(u, 1, axis=0))
        u_up = jnp.where(ridx == rows - 1, u_after, pltpu.roll(u, rows - 1, axis=0))
        cwb = cwb_ref[j]
        c = u_dn * cwb[0:1, :] + u * cwb[1:2, :] + u_up * cwb[2:3, :] + cwb[3:4, :]
        gate, val = c[:, 0:cols], c[:, cols:cols2]
        act_scr[j] = (gate * jax.nn.sigmoid(gate) * val).astype(BF16)

    assert nchunk % 2 == 1
    ua_scr[...] = _dot(h, wup_ref[0])

    def chunk_pair(jj, carry):
        j = 2 * jj
        ua = ua_scr[...]
        ub_scr[...] = _dot(h, wup_ref[j + 1])
        conv_gate(j, ua)
        ub = ub_scr[...]
        ua_scr[...] = _dot(h, wup_ref[j + 2])
        conv_gate(j + 1, ub)
        return carry

    lax.fori_loop(0, nchunk // 2, chunk_pair, 0)
    conv_gate(nchunk - 1, ua_scr[...])
    act =jnp.concatenate([act_scr[j] for j in range(nchunk)], axis=1)
    out = x1_ref[...] + _dot(act, wd_ref[...])
    if final_norm:
        out = _rms(out, gf_ref[...])
    o_ref[...] = out


def _ffn_call(h2, x1, wup_c, cwb_c, wdown, gf, seq, final_norm):
    t, d = x1.shape
    rows = FFN_ROWS
    nchunk, _, cols2 = wup_c.shape
    tiles_per_seq = seq // rows
    halo_per_tile = rows // HALO_ROWS
    last_halo = t // HALO_ROWS - 1
    resident = lambda a: pl.BlockSpec(a.shape, lambda i: (0,) * a.ndim, pipeline_mode=pl.Buffered(1))
    return pl.pallas_call(
        functools.partial(_ffn_kernel, tiles_per_seq=tiles_per_seq, final_norm=final_norm),
        grid=(t // rows,),
        in_specs=[pl.BlockSpec((rows, d), lambda i: (i, 0)),
                  pl.BlockSpec((HALO_ROWS, d), lambda i: (jnp.maximum(i * halo_per_tile - 1, 0), 0)),
                  pl.BlockSpec((HALO_ROWS, d), lambda i: (jnp.minimum((i + 1) * halo_per_tile, last_halo), 0)),
                  pl.BlockSpec((rows, d), lambda i: (i, 0)),
                  resident(wup_c), resident(cwb_c), resident(wdown), resident(gf)],
        out_specs=pl.BlockSpec((rows, d), lambda i: (i, 0)),
        out_shape=jax.ShapeDtypeStruct((t, d), F32),
        scratch_shapes=[pltpu.VMEM((nchunk, rows, cols2 // 2), BF16),
                        pltpu.VMEM((rows + 2 * HALO_ROWS, cols2), F32),
                        pltpu.VMEM((rows + 2 * HALO_ROWS, cols2), F32)],
        compiler_params=_cparams(("parallel",)),
        name="conv_ffn",
    )(h2, h2, h2, x1, wup_c, cwb_c, wdown, gf)


def _prep_ffn(w_up, conv_w, conv_b, cols):
    d, dff2 = w_up.shape
    dff = dff2 // 2
    nchunk = dff // cols

    def chunked(a):
        r = a.shape[0]
        g = a[:, :dff].reshape(r, nchunk, cols)
        v = a[:, dff:].reshape(r, nchunk, cols)
        return jnp.concatenate([g, v], axis=2).transpose(1, 0, 2)

    taps = jnp.concatenate([conv_w, conv_b[None, :], jnp.zeros((8 - CONV_W - 1, dff2), conv_w.dtype)], axis=0)
    return chunked(w_up).astype(BF16), chunked(taps)


def _rotate_half_cols(w):
    half = w.shape[1] // 2
    return jnp.concatenate([-w[:, half:], w[:, :half]], axis=1)


def _prep_layer(w_in, c_w_uq, c_w_ukv, w_out):
    d = w_in.shape[0]
    wa = w_in[:, 0:768]
    wb = w_in[:, 768:1536]
    c_q = w_in[:, 1536:1536 + C_Q_RANK]
    c_kv = w_in[:, 1792:1792 + C_KV_RANK]
    c_kr = w_in[:, 1920:1920 + C_ROPE]
    z = lambda n: jnp.zeros((d, n), w_in.dtype)
    wc = jnp.concatenate([c_q, c_kv, z(64), c_kr, z(32), z(64), _rotate_half_cols(c_kr), z(32)], axis=1)
    uq = c_w_uq.reshape(C_Q_RANK, C_HEADS, C_NOPE + C_ROPE)
    zq = lambda n: jnp.zeros((C_Q_RANK, C_HEADS, n), uq.dtype)
    w1 = jnp.concatenate([uq, zq(32)], axis=2).reshape(C_Q_RANK, C_HEADS * C_HEAD_PAD)
    uq_rope = uq[:, :, C_NOPE:]
    uq_rot = jnp.concatenate([-uq_rope[:, :, C_ROPE // 2:], uq_rope[:, :, :C_ROPE // 2]], axis=2)
    w2 = jnp.concatenate([zq(C_NOPE), uq_rot, zq(32)], axis=2).reshape(C_Q_RANK, C_HEADS * C_HEAD_PAD)
    ukv = c_w_ukv.reshape(C_KV_RANK, C_HEADS, C_NOPE + C_V)
    zk = jnp.zeros((C_KV_RANK, C_HEADS, C_HEAD_PAD - C_NOPE), ukv.dtype)
    wk = jnp.concatenate([ukv[:, :, :C_NOPE], zk], axis=2).reshape(C_KV_RANK, C_HEADS * C_HEAD_PAD)
    wv_pairs = ukv[:, :, C_NOPE:].reshape(C_KV_RANK, C_HEADS // 2, 2 * C_V)
    wv = jnp.concatenate([wv_pairs, jnp.zeros_like(wv_pairs)], axis=2).reshape(C_KV_RANK, C_HEADS * 2 * C_V)
    bf = lambda a: a.astype(BF16)
    return dict(wa=bf(wa), wb=bf(wb), wc=bf(wc), w1=bf(w1), w2=bf(w2), wk=bf(wk), wv=bf(wv),
                woa=bf(w_out[0:256]), wob=bf(w_out[256:512]), woc=bf(w_out[512:1024]))


def _rope_tables(seq):
    pos = jnp.arange(seq, dtype=F32)
    inv_freq = ROPE_BASE ** (-jnp.arange(0, C_ROPE, 2, dtype=F32) / C_ROPE)
    ang = pos[:, None] * inv_freq[None, :]
    cos, sin = jnp.cos(ang), jnp.sin(ang)
    cos2 = jnp.concatenate([cos, cos], axis=1)
    sin2 = jnp.concatenate([sin, sin], axis=1)
    scale = (C_NOPE + C_ROPE) ** -0.5 * LOG2E
    ones = jnp.ones((seq, C_NOPE), F32)
    z = lambda n: jnp.zeros((seq, n), F32)
    cos_head = jnp.concatenate([ones, cos2, z(32)], axis=1)
    sin_head = jnp.concatenate([z(C_NOPE), sin2, z(32)], axis=1)
    cosq = jnp.tile(cos_head, (1, C_HEADS)) * scale
    sinq = jnp.tile(sin_head, (1, C_HEADS)) * scale
    cosk = jnp.concatenate([z(C_NOPE), cos2, z(32)], axis=1)
    sink = sin_head
    return cosq, sinq, cosk, sink


def kernel(x, w_in, g_attn, a_lq1, a_lk1, a_lq2, a_lk2, a_subln, c_g_q, c_w_uq, c_g_kv, c_w_ukv,
           w_out, g_ffn, w_up, conv_w, conv_b, w_down, g_final):
    batch, seq, d = x.shape
    depth = w_in.shape[0]
    assert seq % (16 * BAND_Q_ROWS) == 0 and seq % FFN_ROWS == 0 and d == 1024
    cosq, sinq, cosk, sink = _rope_tables(seq)
    x2 = x.reshape(batch * seq, d)
    row = lambda v: v[None, :]
    for l in range(depth):
        p = _prep_layer(w_in[l], c_w_uq[l], c_w_ukv[l], w_out[l])
        lam_init = 0.8 - 0.6 * math.exp(-0.3 * l)
        qkva, qkvb, qc, kc, vc = _proj_call(
            x2, row(g_attn[l]), p["wa"], p["wb"], p["wc"], row(c_g_q[l]), row(c_g_kv[l]),
            p["w1"], p["w2"], p["wk"], p["wv"], cosq, sinq, cosk, sink, seq)
        a_out = _diff_attn_call(qkva, row(a_lq1[l]), row(a_lk1[l]), row(a_lq2[l]), row(a_lk2[l]),
                                row(jnp.tile(a_subln[l], A_HEADS)), lam_init, batch, seq)
        b_out = _dilated_call(qkvb, batch, seq)
        c_out = _mla_call(qc, kc, vc, batch, seq)
        x1, h2 = _out_call(a_out, b_out, c_out, x2, p["woa"], p["wob"], p["woc"], row(g_ffn[l]))
        wup_c, cwb_c = _prep_ffn(w_up[l], conv_w[l], conv_b[l], FFN_COLS)
        x2 = _ffn_call(h2, x1, wup_c, cwb_c, w_down[l].astype(BF16), row(g_final), seq,
                       final_norm=(l == depth - 1))
    return x2.reshape(batch, seq, d)
```
